```python
import math
import jax
import jax.numpy as jnp
from jax import lax
import numpy as np

D_MODEL = 1024
BATCH = 4
SEQ = 8192
DEPTH = 4

GRID_W = 64
CTX_LEN = 256
N_MIXERS = 2
N_HEADS = 8
HEAD_DIM = 64
V_HEAD_DIM = 2 * HEAD_DIM
Q_BLOCK = 128
ROPE_BASE = 10000.0
ROPE_AXIS_DIM = HEAD_DIM // 2
D_RNN = D_MODEL
N_LRU_BLOCKS = 8
LRU_BLOCK = D_RNN // N_LRU_BLOCKS
CONV_W = 4
CONV_PAD_L = CONV_W // 2
CONV_PAD_R = CONV_W - 1 - CONV_PAD_L
RG_C = 8.0
D_FF = 2816
N_EXPERTS = 8
TOP_K = 2
D_FF_EXPERT = 3584
EPS = 1e-6
N_EVEN = (DEPTH + 1) // 2
N_ODD = DEPTH // 2

kernel_name = 'hybrid_diffattn_rglru_moe_dit'


def rmsnorm(x, g):
    xf = x.astype(jnp.float32)
    xf = xf * lax.rsqrt(jnp.mean(xf * xf, axis=-1, keepdims=True) + EPS)
    return xf.astype(x.dtype) * g


def modulate(x, g, shift, scale):
    return rmsnorm(x, g) * (1.0 + scale) + shift


def axial_rope_tables(n_tokens):
    rows = n_tokens // GRID_W
    row = jnp.repeat(jnp.arange(rows, dtype=jnp.float32), GRID_W)
    col = jnp.tile(jnp.arange(GRID_W, dtype=jnp.float32), rows)
    inv_freq = 1.0 / (ROPE_BASE ** (jnp.arange(0, ROPE_AXIS_DIM, 2, dtype=jnp.float32) / ROPE_AXIS_DIM))
    ang = jnp.stack([row[:, None] * inv_freq, col[:, None] * inv_freq], axis=1)
    ang = jnp.broadcast_to(ang[:, :, None, :], (n_tokens, 2, 2, ROPE_AXIS_DIM // 2))
    ang = ang.reshape(n_tokens, HEAD_DIM)
    return jnp.cos(ang), jnp.sin(ang)


def apply_axial_rope(x, cos, sin):
    xs = x.reshape(*x.shape[:-1], 2, 2, ROPE_AXIS_DIM // 2)
    rot = jnp.stack([-xs[..., 1, :], xs[..., 0, :]], axis=-2).reshape(x.shape)
    cos = cos[None, :, None, None, :].astype(x.dtype)
    sin = sin[None, :, None, None, :].astype(x.dtype)
    return x * cos + rot * sin


def diff_softmax_mix(q, k, v, lam):
    s = jnp.einsum('bqhpd,bkhpd->pbhqk', q, k).astype(jnp.float32)
    p = jax.nn.softmax(s, axis=-1)
    w = (p[0] - lam * p[1]).astype(v.dtype)
    return jnp.einsum('bhqk,bkhe->bqhe', w, v)


def diff_attention(h_lat, h_ctx, w_qkv, w_o, lam_vec, subln_g, lambda_init, cos, sin, with_ctx_out):
    bsz, n_lat, _ = h_lat.shape
    n_ctx = h_ctx.shape[1]
    scale = HEAD_DIM ** -0.5
    q_l, k_l, v_l = jnp.split(h_lat @ w_qkv, 3, axis=-1)
    q_l = apply_axial_rope(q_l.reshape(bsz, n_lat, N_HEADS, 2, HEAD_DIM) * scale, cos, sin)
    k_l = apply_axial_rope(k_l.reshape(bsz, n_lat, N_HEADS, 2, HEAD_DIM), cos, sin)
    v_l = v_l.reshape(bsz, n_lat, N_HEADS, V_HEAD_DIM)
    k_c, v_c = jnp.split(h_ctx @ w_qkv[:, D_MODEL:], 2, axis=-1)
    k_c = k_c.reshape(bsz, n_ctx, N_HEADS, 2, HEAD_DIM)
    v_c = v_c.reshape(bsz, n_ctx, N_HEADS, V_HEAD_DIM)
    lv = lam_vec.astype(jnp.float32)
    lam = jnp.exp(jnp.sum(lv[0] * lv[1])) - jnp.exp(jnp.sum(lv[2] * lv[3])) + lambda_init
    k_all = jnp.concatenate([k_l, k_c], axis=1)
    v_all = jnp.concatenate([v_l, v_c], axis=1)
    n_blocks = n_lat // Q_BLOCK
    q_blocks = q_l.reshape(bsz, n_blocks, Q_BLOCK, N_HEADS, 2, HEAD_DIM).swapaxes(0, 1)
    o_l = lax.map(lambda qb: diff_softmax_mix(qb, k_all, v_all, lam), q_blocks)
    o_l = o_l.swapaxes(0, 1).reshape(bsz, n_lat, N_HEADS, V_HEAD_DIM)

    def head_out(o):
        o = rmsnorm(o, subln_g) * (1.0 - lambda_init)
        return o.reshape(bsz, o.shape[1], D_MODEL) @ w_o

    y_l = head_out(o_l)
    if with_ctx_out:
        q_c = (h_ctx @ w_qkv[:, :D_MODEL]).reshape(bsz, n_ctx, N_HEADS, 2, HEAD_DIM) * scale
        y_c = head_out(diff_softmax_mix(q_c, k_c, v_c, lam))
    else:
        y_c = None
    return y_l, y_c


def depthwise_conv(u, w, b):
    y = lax.conv_general_dilated(u, w[:, None, :], window_strides=(1,),
                                 padding=[(CONV_PAD_L, CONV_PAD_R)],
                                 dimension_numbers=('NWC', 'WIO', 'NWC'),
                                 feature_group_count=u.shape[-1])
    return y + b


def rglru_coeffs(u, w_gates, b_gates, a_param):
    bsz, n, _ = u.shape
    ub = u.reshape(bsz, n, N_LRU_BLOCKS, LRU_BLOCK)
    g = jnp.einsum('bnkc,gkcd->gbnkd', ub, w_gates).reshape(2, bsz, n, D_RNN) + b_gates[:, None, None, :]
    g = jax.nn.sigmoid(g.astype(jnp.float32))
    r_t, i_t = g[0], g[1]
    log_a = RG_C * r_t * jax.nn.log_sigmoid(a_param.astype(jnp.float32))
    a_t = jnp.exp(log_a)
    b_t = jnp.sqrt(-jnp.expm1(2.0 * log_a)) * (i_t * u.astype(jnp.float32))
    return a_t, b_t


def linear_scan(a, b, h0, reverse):
    def combine(e1, e2):
        a1, b1 = e1
        a2, b2 = e2
        return a1 * a2, a2 * b1 + b2
    a_cum, b_cum = lax.associative_scan(combine, (a, b), reverse=reverse, axis=1)
    return a_cum * h0[:, None, :] + b_cum


def rglru_mixer(h_lat, h_ctx, w_in, conv_w, conv_b, gate_w, gate_b, a_param, w_out, with_ctx_out):
    bsz = h_lat.shape[0]
    proj_l = h_lat @ w_in
    proj_c = h_ctx @ w_in
    gate_l = jax.nn.gelu(proj_l[..., :D_RNN])
    u_l = depthwise_conv(proj_l[..., D_RNN:], conv_w, conv_b)
    u_c = depthwise_conv(proj_c[..., D_RNN:], conv_w, conv_b)
    h0 = jnp.zeros((bsz, D_RNN), jnp.float32)
    lat_states, ctx_states = [], []
    for d, reverse in enumerate((False, True)):
        a_c, b_c = rglru_coeffs(u_c, gate_w[d], gate_b[d], a_param[d])
        h_c = linear_scan(a_c, b_c, h0, reverse)
        h_c_final = h_c[:, 0] if reverse else h_c[:, -1]
        a_l, b_l = rglru_coeffs(u_l, gate_w[d], gate_b[d], a_param[d])
        lat_states.append(linear_scan(a_l, b_l, h_c_final, reverse))
        ctx_states.append(h_c)
    y_l = ((lat_states[0] + lat_states[1]).astype(gate_l.dtype) * gate_l) @ w_out
    if with_ctx_out:
        gate_c = jax.nn.gelu(proj_c[..., :D_RNN])
        y_c = ((ctx_states[0] + ctx_states[1]).astype(gate_c.dtype) * gate_c) @ w_out
    else:
        y_c = None
    return y_l, y_c


def swiglu(h, w_gate_up, w_down):
    gate, up = jnp.split(h @ w_gate_up, 2, axis=-1)
    return (jax.nn.silu(gate) * up) @ w_down


def moe_swiglu(h, router_w, w_gate_up, w_down):
    logits = (h @ router_w).astype(jnp.float32)
    top_val, top_idx = lax.top_k(logits, TOP_K)
    top_w = jax.nn.softmax(top_val, axis=-1)
    gates = jnp.einsum('bnk,bnke->bne', top_w,
                       jax.nn.one_hot(top_idx, N_EXPERTS, dtype=jnp.float32)).astype(h.dtype)
    y = jnp.zeros_like(h)
    for e in range(N_EXPERTS):
        y = y + gates[..., e:e + 1] * swiglu(h, w_gate_up[e], w_down[e])
    return y


def setup_inputs(seed: int = 0) -> dict:
    key = jax.random.key(seed)
    ks = jax.random.split(key, 25)
    f32 = jnp.float32

    def nrm(k, shape, scale):
        return jax.random.normal(k, shape, f32) * scale

    D = D_MODEL
    a0 = jax.random.uniform(ks[17], (N_ODD, 2, D_RNN), f32, 0.9, 0.999)
    s = a0 ** (1.0 / RG_C)
    return {
        'x': nrm(ks[0], (BATCH, SEQ, D), 1.0),
        'c': nrm(ks[1], (BATCH, D), 1.0),
        'ctx': nrm(ks[2], (BATCH, CTX_LEN, D), 1.0),
        'c_ctx': nrm(ks[3], (D,), 1.0),
        'mod_w': nrm(ks[4], (DEPTH, D, 6 * D), 0.5 * D ** -0.5),
        'mod_b': nrm(ks[5], (DEPTH, 6 * D), 0.02),
        'norm_mix_g': 1.0 + nrm(ks[6], (DEPTH, D), 0.02),
        'norm_ffn_g': 1.0 + nrm(ks[7], (DEPTH, D), 0.02),
        'attn_w_qkv': nrm(ks[8], (N_EVEN, D, 3 * D), D ** -0.5),
        'attn_w_o': nrm(ks[9], (N_EVEN, D, D), D ** -0.5),
        'attn_lambda': nrm(ks[10], (N_EVEN, 4, HEAD_DIM), 0.1),
        'attn_subln_g': 1.0 + nrm(ks[11], (N_EVEN, V_HEAD_DIM), 0.02),
        'lru_w_in': nrm(ks[12], (N_ODD, D, 2 * D_RNN), D ** -0.5),
        'lru_conv_w': nrm(ks[13], (N_ODD, CONV_W, D_RNN), CONV_W ** -0.5),
        'lru_conv_b': nrm(ks[14], (N_ODD, D_RNN), 0.01),
        'lru_gate_w': nrm(ks[15], (N_ODD, 2, 2, N_LRU_BLOCKS, LRU_BLOCK, LRU_BLOCK), LRU_BLOCK ** -0.5),
        'lru_gate_b': nrm(ks[16], (N_ODD, 2, 2, D_RNN), 0.01),
        'lru_a_param': jnp.log(s) - jnp.log1p(-s),
        'lru_w_out': nrm(ks[18], (N_ODD, D_RNN, D), D_RNN ** -0.5),
        'ffn_w_gate_up': nrm(ks[19], (N_EVEN, D, 2 * D_FF), D ** -0.5),
        'ffn_w_down': nrm(ks[20], (N_EVEN, D_FF, D), D_FF ** -0.5),
        'moe_router_w': nrm(ks[21], (N_ODD, D, N_EXPERTS), D ** -0.5),
        'moe_w_gate_up': nrm(ks[22], (N_ODD, N_EXPERTS, D, 2 * D_FF_EXPERT), D ** -0.5),
        'moe_w_down': nrm(ks[23], (N_ODD, N_EXPERTS, D_FF_EXPERT, D), D_FF_EXPERT ** -0.5),
        'final_norm_g': 1.0 + nrm(ks[24], (D,), 0.02),
    }


def reference(x, c, ctx, c_ctx, mod_w, mod_b, norm_mix_g, norm_ffn_g, attn_w_qkv, attn_w_o,
              attn_lambda, attn_subln_g, lru_w_in, lru_conv_w, lru_conv_b, lru_gate_w, lru_gate_b,
              lru_a_param, lru_w_out, ffn_w_gate_up, ffn_w_down, moe_router_w, moe_w_gate_up,
              moe_w_down, final_norm_g):
    n_lat = x.shape[1]
    cos, sin = axial_rope_tables(n_lat)
    silu_c = jax.nn.silu(c)
    silu_cc = jax.nn.silu(c_ctx)

    def channel_mixer(i, j, f):
        if i % 2 == 0:
            return swiglu(f, ffn_w_gate_up[j], ffn_w_down[j])
        return moe_swiglu(f, moe_router_w[j], moe_w_gate_up[j], moe_w_down[j])

    for i in range(DEPTH):
        last = i == DEPTH - 1
        j = i // N_MIXERS
        mod_l = (silu_c @ mod_w[i] + mod_b[i])[:, None, :]
        mod_c = silu_cc @ mod_w[i] + mod_b[i]
        sh1, sc1, g1, sh2, sc2, g2 = jnp.split(mod_l, 6, axis=-1)
        sh1c, sc1c, g1c, sh2c, sc2c, g2c = jnp.split(mod_c, 6, axis=-1)
        h_l = modulate(x, norm_mix_g[i], sh1, sc1)
        h_c = modulate(ctx, norm_mix_g[i], sh1c, sc1c)
        if i % N_MIXERS == 0:
            lambda_init = 0.8 - 0.6 * math.exp(-0.3 * i)
            y_l, y_c = diff_attention(h_l, h_c, attn_w_qkv[j], attn_w_o[j], attn_lambda[j],
                                      attn_subln_g[j], lambda_init, cos, sin, not last)
        else:
            y_l, y_c = rglru_mixer(h_l, h_c, lru_w_in[j], lru_conv_w[j], lru_conv_b[j], lru_gate_w[j],
                                   lru_gate_b[j], lru_a_param[j], lru_w_out[j], not last)
        x = x + g1 * y_l
        f_l = modulate(x, norm_ffn_g[i], sh2, sc2)
        x = x + g2 * channel_mixer(i, j, f_l)
        if not last:
            ctx = ctx + g1c * y_c
            f_c = modulate(ctx, norm_ffn_g[i], sh2c, sc2c)
            ctx = ctx + g2c * channel_mixer(i, j, f_c)
    return rmsnorm(x, final_norm_g)
```

```python
import functools
import math

import jax
import jax.numpy as jnp
from jax import lax
from jax.experimental import pallas as pl
from jax.experimental.pallas import tpu as pltpu

F32 = jnp.float32
BF16 = jnp.bfloat16

N_HEADS = 8
HEAD_DIM = 64
HEAD_W = 2 * HEAD_DIM
GRID_W = 64
ROPE_BASE = 10000.0
ROPE_AXIS_DIM = HEAD_DIM // 2
N_LRU_BLOCKS = 8
RG_C = 8.0
N_EXPERTS = 8
EPS = 1e-6

LANES = 128
SEG = 256
ATT_TQ = 256
ATT_TK = 768
FFN_TM = 768
FFN_TF = 256
MOE_TM = 512
MOE_TF = 512
MOD_TN = 1536
VMEM_LIMIT = 56 * 1024 * 1024


def _cparams(sem):
    return pltpu.CompilerParams(dimension_semantics=sem, vmem_limit_bytes=VMEM_LIMIT)


def _rms(x):
    return x * lax.rsqrt(jnp.mean(x * x, axis=-1, keepdims=True) + EPS)


def _modulate(x, g, shift, scale):
    return (_rms(x) * g) * (1.0 + scale) + shift


def _mod_row(b, i):
    return jnp.where(i == 0, 4, b)


def _mod_kernel(c_ref, w_ref, b_ref, o_ref):
    c = c_ref[...]
    a = (c * jax.nn.sigmoid(c)).astype(BF16)
    o_ref[0] = jnp.dot(a, w_ref[0].astype(BF16), preferred_element_type=F32) + b_ref[0]


def _modulation(c_pad, mod_w, mod_b):
    depth, d, n = mod_w.shape
    return pl.pallas_call(
        _mod_kernel,
        grid=(depth, n // MOD_TN),
        in_specs=[
            pl.BlockSpec((8, d), lambda l, j: (0, 0)),
            pl.BlockSpec((1, d, MOD_TN), lambda l, j: (l, 0, j)),
            pl.BlockSpec((1, 1, MOD_TN), lambda l, j: (l, 0, j)),
        ],
        out_specs=pl.BlockSpec((1, 8, MOD_TN), lambda l, j: (l, 0, j)),
        out_shape=jax.ShapeDtypeStruct((depth, 8, n), F32),
        compiler_params=_cparams(("arbitrary", "arbitrary")),
        name="modulation",
    )(c_pad, mod_w, mod_b.reshape(depth, 1, n))


def _qkv_kernel(s_ref, g_ref, sh_ref, sc_ref, w_ref, cos_ref, sin_ref, qt_ref, k_ref, vt_ref):
    d = s_ref.shape[-1]
    h = _modulate(s_ref[0], g_ref[...], sh_ref[0], sc_ref[0]).astype(BF16)
    cos = cos_ref[...]
    sin = sin_ref[...]
    lane = lax.broadcasted_iota(jnp.int32, cos.shape, 1)
    first = (lane % ROPE_AXIS_DIM) < (ROPE_AXIS_DIM // 2)

    def rope(x):
        rot = jnp.where(first, pltpu.roll(x, LANES - ROPE_AXIS_DIM // 2, 1),
                        pltpu.roll(x, ROPE_AXIS_DIM // 2, 1))
        return x * cos + rot * sin

    pair = 2 * HEAD_W
    for c in range(N_HEADS // 2):
        rq = jnp.dot(h, w_ref[:, c * pair:(c + 1) * pair], preferred_element_type=F32)
        rk = jnp.dot(h, w_ref[:, d + c * pair:d + (c + 1) * pair], preferred_element_type=F32)
        rv = jnp.dot(h, w_ref[:, 2 * d + c * pair:2 * d + (c + 1) * pair], preferred_element_type=F32)
        for u in range(2):
            hh = 2 * c + u
            sl = slice(u * HEAD_W, (u + 1) * HEAD_W)
            q = rope(rq[:, sl]) * (HEAD_DIM ** -0.5)
            qt_ref[0, hh] = q.T.astype(BF16)
            k_ref[0, hh] = rope(rk[:, sl]).astype(BF16)
            vt_ref[0, hh, 0] = rv[:, sl].T.astype(BF16)


def _qkv(s, g, sh, sc, w, cos, sin):
    b, ntok, d = s.shape
    nseg = ntok // SEG
    per = ATT_TK // SEG
    mod_spec = pl.BlockSpec((1, 1, d), lambda bb, i: (_mod_row(bb, i), 0, 0))
    return pl.pallas_call(
        _qkv_kernel,
        grid=(b, nseg),
        in_specs=[
            pl.BlockSpec((1, SEG, d), lambda bb, i: (bb, i, 0)),
            pl.BlockSpec((1, d), lambda bb, i: (0, 0)),
            mod_spec, mod_spec,
            pl.BlockSpec((d, 3 * d), lambda bb, i: (0, 0)),
            pl.BlockSpec((SEG, HEAD_W), lambda bb, i: (i, 0)),
            pl.BlockSpec((SEG, HEAD_W), lambda bb, i: (i, 0)),
        ],
        out_specs=[
            pl.BlockSpec((1, N_HEADS, HEAD_W, SEG), lambda bb, i: (bb, 0, 0, i)),
            pl.BlockSpec((1, N_HEADS, SEG, HEAD_W), lambda bb, i: (bb, 0, i, 0)),
            pl.BlockSpec((1, N_HEADS, 1, HEAD_W, SEG), lambda bb, i: (bb, 0, i // per, 0, i % per)),
        ],
        out_shape=[
            jax.ShapeDtypeStruct((b, N_HEADS, HEAD_W, ntok), BF16),
            jax.ShapeDtypeStruct((b, N_HEADS, ntok, HEAD_W), BF16),
            jax.ShapeDtypeStruct((b, N_HEADS, ntok // ATT_TK, HEAD_W, ATT_TK), BF16),
        ],
        compiler_params=_cparams(("arbitrary", "arbitrary")),
        name="qkv_rope",
    )(s, g, sh, sc, w, cos, sin)


def _attn_kernel(lam_ref, g_ref, qt_ref, k_ref, vt_ref, o_ref, m_ref, l_ref, acc_ref,
                 *, n_chunks, tk, lambda_init):
    qt = qt_ref[0, 0]
    row = lax.broadcasted_iota(jnp.int32, qt.shape, 0)
    zero = jnp.zeros_like(qt)
    qparts = (jnp.where(row < HEAD_DIM, qt, zero), jnp.where(row >= HEAD_DIM, qt, zero))

    m_ref[...] = jnp.full(m_ref.shape, -jnp.inf, F32)
    l_ref[...] = jnp.zeros(l_ref.shape, F32)
    acc_ref[...] = jnp.zeros(acc_ref.shape, F32)

    def chunk(c, carry):
        kc = k_ref[0, 0, pl.ds(pl.multiple_of(c * tk, tk), tk), :]
        vc = vt_ref[0, 0, c]
        for p in range(2):
            s = jnp.dot(kc, qparts[p], preferred_element_type=F32)
            m_old = m_ref[p]
            m_new = jnp.maximum(m_old, jnp.max(s, axis=0, keepdims=True))
            alpha = jnp.exp(m_old - m_new)
            pr = jnp.exp(s - m_new)
            l_ref[p] = alpha * l_ref[p] + jnp.sum(pr, axis=0, keepdims=True)
            acc_ref[p] = alpha * acc_ref[p] + jnp.dot(vc, pr.astype(BF16), preferred_element_type=F32)
            m_ref[p] = m_new
        return carry

    lax.fori_loop(0, n_chunks, chunk, 0)

    lv = lam_ref[...]
    lam = (jnp.exp(jnp.sum(lv[0:1] * lv[1:2], axis=1, keepdims=True))
           - jnp.exp(jnp.sum(lv[2:3] * lv[3:4], axis=1, keepdims=True)) + lambda_init)
    ot = acc_ref[0] / l_ref[0] - lam * (acc_ref[1] / l_ref[1])
    ot = ot * lax.rsqrt(jnp.mean(ot * ot, axis=0, keepdims=True) + EPS)
    o_ref[0, 0] = ((ot.T * g_ref[...]) * (1.0 - lambda_init)).astype(o_ref.dtype)


def _attention(lam_vec, subln_g, qt, k, vt, *, q_off, nq, n_chunks, tk, lambda_init):
    b, nh, _, ntok = qt.shape
    qb = q_off // ATT_TQ
    kern = functools.partial(_attn_kernel, n_chunks=n_chunks, tk=tk, lambda_init=lambda_init)
    vt_block = (1, 1, n_chunks, HEAD_W, tk)
    return pl.pallas_call(
        kern,
        grid=(b, nh, nq // ATT_TQ),
        in_specs=[
            pl.BlockSpec((4, HEAD_DIM), lambda bb, h, i: (0, 0)),
            pl.BlockSpec((1, HEAD_W), lambda bb, h, i: (0, 0)),
            pl.BlockSpec((1, 1, HEAD_W, ATT_TQ), lambda bb, h, i: (bb, h, 0, i + qb)),
            pl.BlockSpec((1, 1, n_chunks * tk, HEAD_W), lambda bb, h, i: (bb, h, 0, 0)),
            pl.BlockSpec(vt_block, lambda bb, h, i: (bb, h, 0, 0, 0)),
        ],
        out_specs=pl.BlockSpec((1, 1, ATT_TQ, HEAD_W), lambda bb, h, i: (bb, h, i, 0)),
        out_shape=jax.ShapeDtypeStruct((b, nh, nq, HEAD_W), BF16),
        scratch_shapes=[
            pltpu.VMEM((2, 1, ATT_TQ), F32),
            pltpu.VMEM((2, 1, ATT_TQ), F32),
            pltpu.VMEM((2, HEAD_W, ATT_TQ), F32),
        ],
        compiler_params=_cparams(("arbitrary", "arbitrary", "arbitrary")),
        name="diff_attention",
    )(lam_vec, subln_g, qt, k, vt)


def _route_top2(f, rw_ref, tri_ref, cnt_ref):
    logits = jnp.dot(f.astype(BF16), rw_ref[...], preferred_element_type=F32)
    lane = lax.broadcasted_iota(jnp.int32, logits.shape, 1).astype(F32)
    neg = jnp.full(logits.shape, -jnp.inf, F32)
    lg = jnp.where(lane < N_EXPERTS, logits, neg)
    m1 = jnp.max(lg, axis=1, keepdims=True)
    i1 = jnp.min(jnp.where(lg == m1, lane, float(LANES)), axis=1, keepdims=True)
    lg2 = jnp.where(lane == i1, neg, lg)
    m2 = jnp.max(lg2, axis=1, keepdims=True)
    i2 = jnp.min(jnp.where(lg2 == m2, lane, float(LANES)), axis=1, keepdims=True)
    e = jnp.exp(m2 - m1)
    w1 = 1.0 / (1.0 + e)
    w2 = e / (1.0 + e)
    oh1 = (lane == i1).astype(F32)
    oh2 = (lane == i2).astype(F32)
    cnt = oh1 + oh2
    before = jnp.dot(tri_ref[...], cnt.astype(BF16), preferred_element_type=F32) + cnt_ref[0:1, :]
    r1 = jnp.sum(oh1 * before, axis=1, keepdims=True)
    r2 = jnp.sum(oh2 * before, axis=1, keepdims=True)
    cnt_ref[...] = cnt_ref[...] + jnp.sum(cnt, axis=0, keepdims=True)
    meta = jnp.zeros(logits.shape, F32)
    for j, val in enumerate((i1, i2, w1, w2, r1, r2)):
        meta = jnp.where(lane == float(j), val, meta)
    return meta


def _tail(a, s_ref, w_ref, g1_ref, ng_ref, sh_ref, sc_ref, snew_ref, f_ref):
    y = jnp.dot(a, w_ref[...], preferred_element_type=F32)
    s_new = s_ref[0] + g1_ref[0] * y
    snew_ref[0] = s_new
    f = _modulate(s_new, ng_ref[...], sh_ref[0], sc_ref[0])
    f_ref[0] = f.astype(f_ref.dtype)
    return f


def _wo_kernel(oc_ref, ol_ref, s_ref, w_ref, g1_ref, ng_ref, sh_ref, sc_ref, snew_ref, f_ref):
    is_ctx = pl.program_id(1) == 0
    a = jnp.concatenate(
        [jnp.where(is_ctx, oc_ref[0, h], ol_ref[0, h]) for h in range(N_HEADS)], axis=1)
    _tail(a, s_ref, w_ref, g1_ref, ng_ref, sh_ref, sc_ref, snew_ref, f_ref)


def _wo(o_ctx, o_lat, s, w, g1, ng, sh, sc):
    b, ntok, d = s.shape
    nseg = ntok // SEG
    mod_spec = pl.BlockSpec((1, 1, d), lambda bb, i: (_mod_row(bb, i), 0, 0))
    seg_spec = pl.BlockSpec((1, SEG, d), lambda bb, i: (bb, i, 0))
    return pl.pallas_call(
        _wo_kernel,
        grid=(b, nseg),
        in_specs=[
            pl.BlockSpec((1, N_HEADS, SEG, HEAD_W), lambda bb, i: (bb, 0, 0, 0)),
            pl.BlockSpec((1, N_HEADS, SEG, HEAD_W), lambda bb, i: (bb, 0, jnp.maximum(i - 1, 0), 0)),
            seg_spec,
            pl.BlockSpec((d, d), lambda bb, i: (0, 0)),
            mod_spec,
            pl.BlockSpec((1, d), lambda bb, i: (0, 0)),
            mod_spec, mod_spec,
        ],
        out_specs=[seg_spec, seg_spec],
        out_shape=[jax.ShapeDtypeStruct(s.shape, F32), jax.ShapeDtypeStruct(s.shape, BF16)],
        compiler_params=_cparams(("arbitrary", "arbitrary")),
        name="attn_out_proj",
    )(o_ctx, o_lat, s, w, g1, ng, sh, sc)


def _wout_kernel(h0_ref, h1_ref, gate_ref, s_ref, w_ref, g1_ref, ng_ref, sh_ref, sc_ref,
                 rw_ref, tri_ref, snew_ref, f_ref, meta_ref, cnt_ref):
    @pl.when((pl.program_id(0) == 0) & (pl.program_id(1) == 0))
    def _():
        cnt_ref[...] = jnp.zeros(cnt_ref.shape, F32)

    a = ((h0_ref[0, 0] + h1_ref[0, 0]) * gate_ref[0]).astype(BF16)
    f = _tail(a, s_ref, w_ref, g1_ref, ng_ref, sh_ref, sc_ref, snew_ref, f_ref)
    meta_ref[0] = _route_top2(f, rw_ref, tri_ref, cnt_ref)


def _wout(hs, gate, s, w, g1, ng, sh, sc, rw, tri):
    b, ntok, d = s.shape
    nseg = ntok // SEG
    mod_spec = pl.BlockSpec((1, 1, d), lambda bb, i: (_mod_row(bb, i), 0, 0))
    seg_spec = pl.BlockSpec((1, SEG, d), lambda bb, i: (bb, i, 0))
    return pl.pallas_call(
        _wout_kernel,
        grid=(b, nseg),
        in_specs=[
            pl.BlockSpec((1, 1, SEG, d), lambda bb, i: (0, bb, i, 0)),
            pl.BlockSpec((1, 1, SEG, d), lambda bb, i: (1, bb, i, 0)),
            seg_spec, seg_spec,
            pl.BlockSpec((d, d), lambda bb, i: (0, 0)),
            mod_spec,
            pl.BlockSpec((1, d), lambda bb, i: (0, 0)),
            mod_spec, mod_spec,
            pl.BlockSpec((d, LANES), lambda bb, i: (0, 0)),
            pl.BlockSpec((SEG, SEG), lambda bb, i: (0, 0)),
        ],
        out_specs=[
            seg_spec, seg_spec,
            pl.BlockSpec((1, SEG, LANES), lambda bb, i: (bb, i, 0)),
            pl.BlockSpec((8, LANES), lambda bb, i: (0, 0)),
        ],
        out_shape=[
            jax.ShapeDtypeStruct(s.shape, F32),
            jax.ShapeDtypeStruct(s.shape, F32),
            jax.ShapeDtypeStruct((b, ntok, LANES), F32),
            jax.ShapeDtypeStruct((8, LANES), F32),
        ],
        compiler_params=_cparams(("arbitrary", "arbitrary")),
        name="lru_out_proj_router",
    )(hs, hs, gate, s, w, g1, ng, sh, sc, rw, tri)


def _swiglu_acc(x, wg_ref, wu_ref, wd_ref, acc_ref, j):
    @pl.when(j == 0)
    def _():
        acc_ref[...] = jnp.zeros(acc_ref.shape, F32)

    hg = jnp.dot(x, wg_ref[0], preferred_element_type=F32)
    hu = jnp.dot(x, wu_ref[0], preferred_element_type=F32)
    a = ((hg * jax.nn.sigmoid(hg)) * hu).astype(BF16)
    acc_ref[...] += jnp.dot(a, wd_ref[0], preferred_element_type=F32)


def _ffn_dense_kernel(f_ref, wg_ref, wu_ref, wd_ref, s_ref, gl_ref, gc_ref, o_ref, acc_ref):
    j = pl.program_id(2)
    _swiglu_acc(f_ref[0], wg_ref, wu_ref, wd_ref, acc_ref, j)

    @pl.when(j == pl.num_programs(2) - 1)
    def _():
        rows = lax.broadcasted_iota(jnp.int32, acc_ref.shape, 0)
        is_ctx = (rows < SEG) & (pl.program_id(1) == 0)
        gate = jnp.where(is_ctx, gc_ref[0], gl_ref[0])
        o_ref[0] = s_ref[0] + gate * acc_ref[...]


def _ffn_dense(f, wgu, wd, s, g2):
    b, ntok, d = s.shape
    ff = wd.shape[1]
    nf = ff // FFN_TF
    row_spec = pl.BlockSpec((1, FFN_TM, d), lambda bb, i, j: (bb, i, 0))
    return pl.pallas_call(
        _ffn_dense_kernel,
        grid=(b, ntok // FFN_TM, nf),
        in_specs=[
            row_spec,
            pl.BlockSpec((1, d, FFN_TF), lambda bb, i, j: (0, 0, j)),
            pl.BlockSpec((1, d, FFN_TF), lambda bb, i, j: (0, 0, j + nf)),
            pl.BlockSpec((1, FFN_TF, d), lambda bb, i, j: (0, j, 0)),
            row_spec,
            pl.BlockSpec((1, 1, d), lambda bb, i, j: (bb, 0, 0)),
            pl.BlockSpec((1, 1, d), lambda bb, i, j: (4, 0, 0)),
        ],
        out_specs=row_spec,
        out_shape=jax.ShapeDtypeStruct(s.shape, F32),
        scratch_shapes=[pltpu.VMEM((FFN_TM, d), F32)],
        compiler_params=_cparams(("arbitrary", "arbitrary", "arbitrary")),
        name="ffn_dense",
    )(f, wgu, wgu, wd, s, g2, g2)


def _ffn_moe_kernel(te_ref, nt_ref, x_ref, wg_ref, wu_ref, wd_ref, o_ref, acc_ref):
    i = pl.program_id(0)
    j = pl.program_id(1)

    @pl.when(i < nt_ref[0])
    def _():
        _swiglu_acc(x_ref[...].astype(BF16), wg_ref, wu_ref, wd_ref, acc_ref, j)

    @pl.when(j == pl.num_programs(1) - 1)
    def _():
        o_ref[...] = acc_ref[...]


def _ffn_moe(tile_expert, n_tiles, xs, wgu, wd):
    p, d = xs.shape
    ff = wd.shape[1]
    nf = ff // MOE_TF
    row_spec = pl.BlockSpec((MOE_TM, d), lambda i, j, te, nt: (i, 0))
    grid_spec = pltpu.PrefetchScalarGridSpec(
        num_scalar_prefetch=2,
        grid=(p // MOE_TM, nf),
        in_specs=[
            row_spec,
            pl.BlockSpec((1, d, MOE_TF), lambda i, j, te, nt: (te[i], 0, j)),
            pl.BlockSpec((1, d, MOE_TF), lambda i, j, te, nt: (te[i], 0, j + nf)),
            pl.BlockSpec((1, MOE_TF, d), lambda i, j, te, nt: (te[i], j, 0)),
        ],
        out_specs=row_spec,
        scratch_shapes=[pltpu.VMEM((MOE_TM, d), F32)],
    )
    return pl.pallas_call(
        _ffn_moe_kernel,
        grid_spec=grid_spec,
        out_shape=jax.ShapeDtypeStruct((p, d), F32),
        compiler_params=_cparams(("arbitrary", "arbitrary")),
        name="ffn_experts",
    )(tile_expert, n_tiles, xs, wgu, wgu, wd)


def _row_copy(src_ref, src_row, dst_ref, dst_row, sem):
    return pltpu.make_async_copy(src_ref.at[pl.ds(src_row, 1), :], dst_ref.at[pl.ds(dst_row, 1), :], sem)


def _dispatch_kernel(pos_ref, f_ref, xs_in_ref, xs_ref, pos_smem, sem_pos, sem_rows):
    del xs_in_ref
    cp = pltpu.make_async_copy(pos_ref.at[0, 0], pos_smem, sem_pos)
    cp.start()
    cp.wait()

    def issue(r, c):
        _row_copy(f_ref, r, xs_ref, pos_smem[2 * r], sem_rows).start()
        _row_copy(f_ref, r, xs_ref, pos_smem[2 * r + 1], sem_rows).start()
        return c

    lax.fori_loop(0, SEG, issue, 0)

    def drain(r, c):
        _row_copy(f_ref, 0, xs_ref, 0, sem_rows).wait()
        _row_copy(f_ref, 0, xs_ref, 0, sem_rows).wait()
        return c

    lax.fori_loop(0, SEG, drain, 0)


def _dispatch(pos, f_rows, xs_init):
    t, d = f_rows.shape
    return pl.pallas_call(
        _dispatch_kernel,
        grid=(t // SEG,),
        in_specs=[
            pl.BlockSpec((1, 1, 2 * SEG), lambda i: (i, 0, 0)),
            pl.BlockSpec((SEG, d), lambda i: (i, 0)),
            pl.BlockSpec(memory_space=pl.ANY),
        ],
        out_specs=pl.BlockSpec(memory_space=pl.ANY),
        out_shape=jax.ShapeDtypeStruct(xs_init.shape, F32),
        scratch_shapes=[
            pltpu.SMEM((2 * SEG,), jnp.int32),
            pltpu.SemaphoreType.DMA,
            pltpu.SemaphoreType.DMA,
        ],
        input_output_aliases={2: 0},
        compiler_params=_cparams(("arbitrary",)),
        name="moe_dispatch",
    )(pos, f_rows, xs_init)


def _combine_kernel(pos_ref, meta_ref, s_ref, g2_ref, fg_ref, ys_ref, o_ref,
                    pos_smem, buf_ref, sem_pos, sem_rows, *, final):
    cp = pltpu.make_async_copy(pos_ref.at[0, 0], pos_smem, sem_pos)
    cp.start()
    cp.wait()

    def issue(r, c):
        _row_copy(ys_ref, pos_smem[2 * r], buf_ref.at[0], r, sem_rows).start()
        _row_copy(ys_ref, pos_smem[2 * r + 1], buf_ref.at[1], r, sem_rows).start()
        return c

    lax.fori_loop(0, SEG, issue, 0)

    def drain(r, c):
        _row_copy(ys_ref, 0, buf_ref.at[0], 0, sem_rows).wait()
        _row_copy(ys_ref, 0, buf_ref.at[1], 0, sem_rows).wait()
        return c

    lax.fori_loop(0, SEG, drain, 0)

    meta = meta_ref[0]
    y = meta[:, 2:3] * buf_ref[0] + meta[:, 3:4] * buf_ref[1]
    s_new = s_ref[0] + g2_ref[0] * y
    if final:
        s_new = _rms(s_new) * fg_ref[...]
    o_ref[0] = s_new


def _combine(pos, meta, s, g2, final_g, ys, *, final):
    b, ntok, d = s.shape
    nseg = ntok // SEG
    off = 1 if final else 0
    nout = nseg - off
    kern = functools.partial(_combine_kernel, final=final)
    return pl.pallas_call(
        kern,
        grid=(b, nout),
        in_specs=[
            pl.BlockSpec((1, 1, 2 * SEG), lambda bb, i: (bb * nseg + i + off, 0, 0)),
            pl.BlockSpec((1, SEG, LANES), lambda bb, i: (bb, i + off, 0)),
            pl.BlockSpec((1, SEG, d), lambda bb, i: (bb, i + off, 0)),
            pl.BlockSpec((1, 1, d), lambda bb, i: (_mod_row(bb, i + off), 0, 0)),
            pl.BlockSpec((1, d), lambda bb, i: (0, 0)),
            pl.BlockSpec(memory_space=pl.ANY),
        ],
        out_specs=pl.BlockSpec((1, SEG, d), lambda bb, i: (bb, i, 0)),
        out_shape=jax.ShapeDtypeStruct((b, nout * SEG, d), F32),
        scratch_shapes=[
            pltpu.SMEM((2 * SEG,), jnp.int32),
            pltpu.VMEM((2, SEG, d), F32),
            pltpu.SemaphoreType.DMA,
            pltpu.SemaphoreType.DMA,
        ],
        compiler_params=_cparams(("arbitrary", "arbitrary")),
        name="moe_combine",
    )(pos, meta, s, g2, final_g, ys)


def _win_kernel(s_ref, g_ref, sh_ref, sc_ref, w_ref, gate_ref, u_ref):
    d = s_ref.shape[-1]
    h = _modulate(s_ref[0], g_ref[...], sh_ref[0], sc_ref[0]).astype(BF16)
    gate_ref[0] = jax.nn.gelu(jnp.dot(h, w_ref[:, :d], preferred_element_type=F32))
    u_ref[0] = jnp.dot(h, w_ref[:, d:], preferred_element_type=F32)


def _win(s, g, sh, sc, w):
    b, ntok, d = s.shape
    mod_spec = pl.BlockSpec((1, 1, d), lambda bb, i: (_mod_row(bb, i), 0, 0))
    seg_spec = pl.BlockSpec((1, SEG, d), lambda bb, i: (bb, i, 0))
    return pl.pallas_call(
        _win_kernel,
        grid=(b, ntok // SEG),
        in_specs=[
            seg_spec,
            pl.BlockSpec((1, d), lambda bb, i: (0, 0)),
            mod_spec, mod_spec,
            pl.BlockSpec((d, 2 * d), lambda bb, i: (0, 0)),
        ],
        out_specs=[seg_spec, seg_spec],
        out_shape=[jax.ShapeDtypeStruct(s.shape, F32), jax.ShapeDtypeStruct(s.shape, F32)],
        compiler_params=_cparams(("arbitrary", "arbitrary")),
        name="lru_in_proj",
    )(s, g, sh, sc, w)


def _lru_block(direction, step, nseg):
    return jnp.where((direction == 0) | (step == 0), step, nseg - step)


def _lru_kernel(u_ref, up_ref, un_ref, cw_ref, cb_ref, gw_ref, gb_ref, ap_ref, h_ref,
                a_ref, b_ref, st_ref, *, nseg):
    direction = pl.program_id(0)
    step = pl.program_id(2)
    blk = _lru_block(direction, step, nseg)
    d = u_ref.shape[-1]

    u = u_ref[0]
    has_prev = blk > 1
    has_next = (blk > 0) & (blk < nseg - 1)
    zero_row = jnp.zeros((1, d), F32)
    prev2 = jnp.where(has_prev, up_ref[0, 6:7, :], zero_row)
    prev1 = jnp.where(has_prev, up_ref[0, 7:8, :], zero_row)
    next1 = jnp.where(has_next, un_ref[0, 0:1, :], zero_row)
    rows = lax.broadcasted_iota(jnp.int32, u.shape, 0)
    um1 = jnp.where(rows == 0, prev1, pltpu.roll(u, 1, 0))
    um2 = jnp.where(rows == 0, prev2, jnp.where(rows == 1, prev1, pltpu.roll(u, 2, 0)))
    up1 = jnp.where(rows == SEG - 1, next1, pltpu.roll(u, SEG - 1, 0))
    cw = cw_ref[...]
    uc = cw[0:1] * um2 + cw[1:2] * um1 + cw[2:3] * u + cw[3:4] * up1 + cb_ref[...]

    ub = uc.astype(BF16)
    blkw = d // N_LRU_BLOCKS
    gr, gi = [], []
    for kb in range(N_LRU_BLOCKS):
        g = jnp.dot(ub[:, kb * blkw:(kb + 1) * blkw], gw_ref[0, kb], preferred_element_type=F32)
        gr.append(g[:, :blkw])
        gi.append(g[:, blkw:])
    gb = gb_ref[0]
    r_t = jax.nn.sigmoid(jnp.concatenate(gr, axis=1) + gb[0:1])
    i_t = jax.nn.sigmoid(jnp.concatenate(gi, axis=1) + gb[1:2])
    ap = ap_ref[0]
    log_sig = jnp.minimum(ap, 0.0) - jnp.log1p(jnp.exp(-jnp.abs(ap)))
    log_a = RG_C * r_t * log_sig
    a_t = jnp.exp(log_a)
    a_ref[...] = a_t
    b_ref[...] = jnp.sqrt(1.0 - a_t * a_t) * (i_t * uc)

    @pl.when(step == 0)
    def _():
        st_ref[...] = jnp.zeros(st_ref.shape, F32)

    def run(reverse):
        def body(t, hprev):
            tt = SEG - 1 - t if reverse else t
            hnew = a_ref[pl.ds(tt, 1), :] * hprev + b_ref[pl.ds(tt, 1), :]
            h_ref[0, 0, pl.ds(tt, 1), :] = hnew
            return hnew
        st_ref[...] = lax.fori_loop(0, SEG, body, st_ref[...], unroll=8)

    @pl.when(direction == 0)
    def _():
        run(False)

    @pl.when(direction == 1)
    def _():
        run(True)


def _lru(u, conv_w, conv_b, gate_w, gate_b, a_param):
    b, ntok, d = u.shape
    nseg = ntok // SEG
    sub = SEG // 8
    nsub = ntok // 8
    blkw = d // N_LRU_BLOCKS

    def blk(dd, i):
        return _lru_block(dd, i, nseg)

    kern = functools.partial(_lru_kernel, nseg=nseg)
    return pl.pallas_call(
        kern,
        grid=(2, b, nseg),
        in_specs=[
            pl.BlockSpec((1, SEG, d), lambda dd, bb, i: (bb, blk(dd, i), 0)),
            pl.BlockSpec((1, 8, d), lambda dd, bb, i: (bb, jnp.maximum(blk(dd, i) * sub - 1, 0), 0)),
            pl.BlockSpec((1, 8, d), lambda dd, bb, i: (bb, jnp.minimum((blk(dd, i) + 1) * sub, nsub - 1), 0)),
            pl.BlockSpec((4, d), lambda dd, bb, i: (0, 0)),
            pl.BlockSpec((1, d), lambda dd, bb, i: (0, 0)),
            pl.BlockSpec((1, N_LRU_BLOCKS, blkw, 2 * blkw), lambda dd, bb, i: (dd, 0, 0, 0)),
            pl.BlockSpec((1, 2, d), lambda dd, bb, i: (dd, 0, 0)),
            pl.BlockSpec((1, 1, d), lambda dd, bb, i: (dd, 0, 0)),
        ],
        out_specs=pl.BlockSpec((1, 1, SEG, d), lambda dd, bb, i: (dd, bb, blk(dd, i), 0)),
        out_shape=jax.ShapeDtypeStruct((2, b, ntok, d), F32),
        scratch_shapes=[
            pltpu.VMEM((SEG, d), F32),
            pltpu.VMEM((SEG, d), F32),
            pltpu.VMEM((1, d), F32),
        ],
        compiler_params=_cparams(("arbitrary", "arbitrary", "arbitrary")),
        name="rglru_scan",
    )(u, u, u, conv_w, conv_b, gate_w, gate_b, a_param)


def _rope_tables(n_ctx, n_lat):
    rows = n_lat // GRID_W
    row = jnp.repeat(jnp.arange(rows, dtype=F32), GRID_W)
    col = jnp.tile(jnp.arange(GRID_W, dtype=F32), rows)
    inv_freq = 1.0 / (ROPE_BASE ** (jnp.arange(0, ROPE_AXIS_DIM, 2, dtype=F32) / ROPE_AXIS_DIM))
    ang = jnp.stack([row[:, None] * inv_freq, col[:, None] * inv_freq], axis=1)
    ang = jnp.broadcast_to(ang[:, :, None, :], (n_lat, 2, 2, ROPE_AXIS_DIM // 2)).reshape(n_lat, HEAD_DIM)
    ang = jnp.concatenate([jnp.zeros((n_ctx, HEAD_DIM), F32), ang], axis=0)
    cos = jnp.cos(ang)
    sin = jnp.sin(ang)
    lane = jnp.arange(HEAD_DIM)
    sin = jnp.where((lane % ROPE_AXIS_DIM) < (ROPE_AXIS_DIM // 2), -sin, sin)
    return jnp.tile(cos, (1, 2)), jnp.tile(sin, (1, 2))


def kernel(x, c, ctx, c_ctx, mod_w, mod_b, norm_mix_g, norm_ffn_g, attn_w_qkv, attn_w_o, attn_lambda, attn_subln_g, lru_w_in, lru_conv_w, lru_conv_b, lru_gate_w, lru_gate_b, lru_a_param, lru_w_out, ffn_w_gate_up, ffn_w_down, moe_router_w, moe_w_gate_up, moe_w_down, final_norm_g):
    bsz, n_lat, d = x.shape
    n_ctx = ctx.shape[1]
    depth = mod_w.shape[0]
    assert n_ctx == SEG and n_lat % ATT_TQ == 0 and bsz <= 4
    ntok = n_ctx + n_lat
    assert ntok % ATT_TK == 0 and ntok % FFN_TM == 0
    nseg = ntok // SEG
    t_all = bsz * ntok

    s = jnp.concatenate([ctx, x], axis=1)
    c_pad = jnp.zeros((8, d), F32).at[:bsz].set(c).at[4].set(c_ctx)
    mod = _modulation(c_pad, mod_w, mod_b)
    cos, sin = _rope_tables(n_ctx, n_lat)
    tri = (jnp.arange(SEG)[:, None] > jnp.arange(SEG)[None, :]).astype(BF16)
    blkw = d // N_LRU_BLOCKS
    out = None

    for i in range(depth):
        j = i // 2
        last = i == depth - 1
        sh1, sc1, g1, sh2, sc2, g2 = [mod[i, :, k * d:(k + 1) * d].reshape(8, 1, d) for k in range(6)]
        ng_mix = norm_mix_g[i].reshape(1, d)
        ng_ffn = norm_ffn_g[i].reshape(1, d)
        if i % 2 == 0:
            lambda_init = 0.8 - 0.6 * math.exp(-0.3 * i)
            qt, k, vt = _qkv(s, ng_mix, sh1, sc1, attn_w_qkv[j].astype(BF16), cos, sin)
            subln = attn_subln_g[j].reshape(1, HEAD_W)
            o_ctx = _attention(attn_lambda[j], subln, qt, k, vt, q_off=0, nq=n_ctx,
                               n_chunks=1, tk=SEG, lambda_init=lambda_init)
            o_lat = _attention(attn_lambda[j], subln, qt, k, vt, q_off=n_ctx, nq=n_lat,
                               n_chunks=ntok // ATT_TK, tk=ATT_TK, lambda_init=lambda_init)
            s, f = _wo(o_ctx, o_lat, s, attn_w_o[j].astype(BF16), g1, ng_ffn, sh2, sc2)
            s = _ffn_dense(f, ffn_w_gate_up[j].astype(BF16)[None], ffn_w_down[j].astype(BF16)[None], s, g2)
        else:
            gate, u = _win(s, ng_mix, sh1, sc1, lru_w_in[j].astype(BF16))
            gw = lru_gate_w[j].transpose(0, 2, 3, 1, 4).reshape(2, N_LRU_BLOCKS, blkw, 2 * blkw).astype(BF16)
            hs = _lru(u, lru_conv_w[j], lru_conv_b[j].reshape(1, d), gw, lru_gate_b[j],
                      lru_a_param[j].reshape(2, 1, d))
            rw = jnp.zeros((d, LANES), BF16).at[:, :N_EXPERTS].set(moe_router_w[j].astype(BF16))
            s, f, meta, cnt = _wout(hs, gate, s, lru_w_out[j].astype(BF16), g1, ng_ffn, sh2, sc2, rw, tri)

            meta2 = meta.reshape(t_all, LANES)
            idx = meta2[:, 0:2].astype(jnp.int32)
            rank = meta2[:, 4:6].astype(jnp.int32)
            counts = cnt[0, :N_EXPERTS].astype(jnp.int32)
            padded = ((counts + MOE_TM - 1) // MOE_TM) * MOE_TM
            ends = jnp.cumsum(padded)
            starts = ends - padded
            pos = (starts[idx] + rank).reshape(t_all // SEG, 1, 2 * SEG)
            p_rows = 2 * t_all + N_EXPERTS * MOE_TM
            tile_start = jnp.arange(p_rows // MOE_TM, dtype=jnp.int32) * MOE_TM
            tile_expert = jnp.minimum(jnp.searchsorted(ends, tile_start, side="right"),
                                      N_EXPERTS - 1).astype(jnp.int32)
            n_tiles = (ends[-1:] // MOE_TM).astype(jnp.int32)

            xs = _dispatch(pos, f.reshape(t_all, d), jnp.zeros((p_rows, d), F32))
            ys = _ffn_moe(tile_expert, n_tiles, xs, moe_w_gate_up[j].astype(BF16), moe_w_down[j].astype(BF16))
            res = _combine(pos, meta, s, g2, final_norm_g.reshape(1, d), ys, final=last)
            if last:
                out = res
            else:
                s = res
    return out
```

```python
import functools
import math

import jax
import jax.numpy as jnp
from jax import lax
from jax.experimental import pallas as pl
from jax.experimental.pallas import tpu as pltpu

F32 = jnp.float32
BF16 = jnp.bfloat16

N_HEADS = 8
HEAD_DIM = 64
HEAD_W = 2 * HEAD_DIM
GRID_W = 64
ROPE_BASE = 10000.0
ROPE_AXIS_DIM = HEAD_DIM // 2
N_LRU_BLOCKS = 8
RG_C = 8.0
N_EXPERTS = 8
EPS = 1e-6
QK_SCALE = HEAD_DIM ** -0.5 * math.log2(math.e)

LANES = 128
SEG = 256
ATT_TQ = 256
ATT_TK = 768
FFN_TM = 768
FFN_TF = 1408
MOE_TM = 512
MOE_TF = 1792
MOD_TN = 1536
VMEM_LIMIT = 56 * 1024 * 1024


def _cparams(sem):
    return pltpu.CompilerParams(dimension_semantics=sem, vmem_limit_bytes=VMEM_LIMIT)


def _rms(x):
    return x * lax.rsqrt(jnp.mean(x * x, axis=-1, keepdims=True) + EPS)


def _modulate(x, g, shift, scale):
    return (_rms(x) * g) * (1.0 + scale) + shift


def _mod_row(b, i):
    return jnp.where(i == 0, 4, b)


def _mod_kernel(c_ref, w_ref, b_ref, o_ref):
    c = c_ref[...]
    a = (c * jax.nn.sigmoid(c)).astype(BF16)
    o_ref[0] = jnp.dot(a, w_ref[0].astype(BF16), preferred_element_type=F32) + b_ref[0]


def _modulation(c_pad, mod_w, mod_b):
    depth, d, n = mod_w.shape
    return pl.pallas_call(
        _mod_kernel,
        grid=(depth, n // MOD_TN),
        in_specs=[
            pl.BlockSpec((8, d), lambda l, j: (0, 0)),
            pl.BlockSpec((1, d, MOD_TN), lambda l, j: (l, 0, j)),
            pl.BlockSpec((1, 1, MOD_TN), lambda l, j: (l, 0, j)),
        ],
        out_specs=pl.BlockSpec((1, 8, MOD_TN), lambda l, j: (l, 0, j)),
        out_shape=jax.ShapeDtypeStruct((depth, 8, n), F32),
        compiler_params=_cparams(("arbitrary", "arbitrary")),
        name="modulation",
    )(c_pad, mod_w, mod_b.reshape(depth, 1, n))


def _qkv_kernel(s_ref, g_ref, sh_ref, sc_ref, w_ref, cos_ref, sin_ref, qt_ref, k_ref, vt_ref):
    d = s_ref.shape[-1]
    h = _modulate(s_ref[0], g_ref[...], sh_ref[0], sc_ref[0]).astype(BF16)
    cos = cos_ref[...]
    sin = sin_ref[...]
    lane = lax.broadcasted_iota(jnp.int32, cos.shape, 1)
    first = (lane % ROPE_AXIS_DIM) < (ROPE_AXIS_DIM // 2)

    def rope(x):
        rot = jnp.where(first, pltpu.roll(x, LANES - ROPE_AXIS_DIM // 2, 1),
                        pltpu.roll(x, ROPE_AXIS_DIM // 2, 1))
        return x * cos + rot * sin

    pair = 2 * HEAD_W
    for c in range(N_HEADS // 2):
        rq = jnp.dot(h, w_ref[:, c * pair:(c + 1) * pair], preferred_element_type=F32)
        rk = jnp.dot(h, w_ref[:, d + c * pair:d + (c + 1) * pair], preferred_element_type=F32)
        rv = jnp.dot(h, w_ref[:, 2 * d + c * pair:2 * d + (c + 1) * pair], preferred_element_type=F32)
        for u in range(2):
            hh = 2 * c + u
            sl = slice(u * HEAD_W, (u + 1) * HEAD_W)
            q = rope(rq[:, sl]) * QK_SCALE
            qt_ref[0, hh] = q.T.astype(BF16)
            k_ref[0, hh] = rope(rk[:, sl]).astype(BF16)
            vt_ref[0, hh, 0] = rv[:, sl].T.astype(BF16)


def _qkv(s, g, sh, sc, w, cos, sin):
    b, ntok, d = s.shape
    nseg = ntok // SEG
    per = ATT_TK // SEG
    mod_spec = pl.BlockSpec((1, 1, d), lambda bb, i: (_mod_row(bb, i), 0, 0))
    return pl.pallas_call(
        _qkv_kernel,
        grid=(b, nseg),
        in_specs=[
            pl.BlockSpec((1, SEG, d), lambda bb, i: (bb, i, 0)),
            pl.BlockSpec((1, d), lambda bb, i: (0, 0)),
            mod_spec, mod_spec,
            pl.BlockSpec((d, 3 * d), lambda bb, i: (0, 0)),
            pl.BlockSpec((SEG, HEAD_W), lambda bb, i: (i, 0)),
            pl.BlockSpec((SEG, HEAD_W), lambda bb, i: (i, 0)),
        ],
        out_specs=[
            pl.BlockSpec((1, N_HEADS, HEAD_W, SEG), lambda bb, i: (bb, 0, 0, i)),
            pl.BlockSpec((1, N_HEADS, SEG, HEAD_W), lambda bb, i: (bb, 0, i, 0)),
            pl.BlockSpec((1, N_HEADS, 1, HEAD_W, SEG), lambda bb, i: (bb, 0, i // per, 0, i % per)),
        ],
        out_shape=[
            jax.ShapeDtypeStruct((b, N_HEADS, HEAD_W, ntok), BF16),
            jax.ShapeDtypeStruct((b, N_HEADS, ntok, HEAD_W), BF16),
            jax.ShapeDtypeStruct((b, N_HEADS, ntok // ATT_TK, HEAD_W, ATT_TK), BF16),
        ],
        compiler_params=_cparams(("arbitrary", "arbitrary")),
        name="qkv_rope",
    )(s, g, sh, sc, w, cos, sin)


def _attn_kernel(lam_ref, g_ref, qt_ref, k_ref, vt_ref, o_ref, sa_ref, sb_ref, l_ref, acc_ref,
                 *, n_chunks, tk, lambda_init):
    qt = qt_ref[0, 0]
    row = lax.broadcasted_iota(jnp.int32, qt.shape, 0)
    zero = jnp.zeros_like(qt)
    qparts = (jnp.where(row < HEAD_DIM, qt, zero), jnp.where(row >= HEAD_DIM, qt, zero))

    l_ref[...] = jnp.zeros(l_ref.shape, F32)
    acc_ref[...] = jnp.zeros(acc_ref.shape, F32)

    def stage_a(c, s_ref, m_run):
        kc = k_ref[0, 0, pl.ds(pl.multiple_of(c * tk, tk), tk), :]
        m_new, alpha = [], []
        for p in range(2):
            s = jnp.dot(kc, qparts[p], preferred_element_type=F32)
            s_ref[p] = s
            mx = jnp.maximum(m_run[p], jnp.max(s, axis=0, keepdims=True))
            alpha.append(jnp.exp2(m_run[p] - mx))
            m_new.append(mx)
        return tuple(m_new), tuple(alpha)

    def stage_b(c, s_ref, m_run, alpha):
        vc = vt_ref[0, 0, c]
        for p in range(2):
            pr = jnp.exp2(s_ref[p] - m_run[p])
            l_ref[p] = alpha[p] * l_ref[p] + jnp.sum(pr, axis=0, keepdims=True)
            acc_ref[p] = alpha[p] * acc_ref[p] + jnp.dot(vc, pr.astype(BF16), preferred_element_type=F32)

    m0 = jnp.full((1, qt.shape[1]), -jnp.inf, F32)
    m_run, alpha = stage_a(0, sa_ref, (m0, m0))

    def pair(i, carry):
        m_run, alpha = carry
        m_nxt, a_nxt = stage_a(2 * i + 1, sb_ref, m_run)
        stage_b(2 * i, sa_ref, m_run, alpha)
        m_run, alpha = stage_a(2 * i + 2, sa_ref, m_nxt)
        stage_b(2 * i + 1, sb_ref, m_nxt, a_nxt)
        return m_run, alpha

    n_pairs = (n_chunks - 1) // 2
    if n_pairs:
        m_run, alpha = lax.fori_loop(0, n_pairs, pair, (m_run, alpha))
    if (n_chunks - 1) % 2:
        m_nxt, a_nxt = stage_a(n_chunks - 1, sb_ref, m_run)
        stage_b(n_chunks - 2, sa_ref, m_run, alpha)
        stage_b(n_chunks - 1, sb_ref, m_nxt, a_nxt)
    else:
        stage_b(n_chunks - 1, sa_ref, m_run, alpha)

    lv = lam_ref[...]
    lam = (jnp.exp(jnp.sum(lv[0:1] * lv[1:2], axis=1, keepdims=True))
           - jnp.exp(jnp.sum(lv[2:3] * lv[3:4], axis=1, keepdims=True)) + lambda_init)
    ot = acc_ref[0] / l_ref[0] - lam * (acc_ref[1] / l_ref[1])
    ot = ot * lax.rsqrt(jnp.mean(ot * ot, axis=0, keepdims=True) + EPS)
    o_ref[0, 0] = ((ot.T * g_ref[...]) * (1.0 - lambda_init)).astype(o_ref.dtype)


def _attention(lam_vec, subln_g, qt, k, vt, *, q_off, nq, n_chunks, tk, lambda_init):
    b, nh, _, ntok = qt.shape
    qb = q_off // ATT_TQ
    kern = functools.partial(_attn_kernel, n_chunks=n_chunks, tk=tk, lambda_init=lambda_init)
    vt_block = (1, 1, n_chunks, HEAD_W, tk)
    return pl.pallas_call(
        kern,
        grid=(b, nh, nq // ATT_TQ),
        in_specs=[
            pl.BlockSpec((4, HEAD_DIM), lambda bb, h, i: (0, 0)),
            pl.BlockSpec((1, HEAD_W), lambda bb, h, i: (0, 0)),
            pl.BlockSpec((1, 1, HEAD_W, ATT_TQ), lambda bb, h, i: (bb, h, 0, i + qb)),
            pl.BlockSpec((1, 1, n_chunks * tk, HEAD_W), lambda bb, h, i: (bb, h, 0, 0)),
            pl.BlockSpec(vt_block, lambda bb, h, i: (bb, h, 0, 0, 0)),
        ],
        out_specs=pl.BlockSpec((1, 1, ATT_TQ, HEAD_W), lambda bb, h, i: (bb, h, i, 0)),
        out_shape=jax.ShapeDtypeStruct((b, nh, nq, HEAD_W), BF16),
        scratch_shapes=[
            pltpu.VMEM((2, tk, ATT_TQ), F32),
            pltpu.VMEM((2, tk, ATT_TQ), F32),
            pltpu.VMEM((2, 1, ATT_TQ), F32),
            pltpu.VMEM((2, HEAD_W, ATT_TQ), F32),
        ],
        compiler_params=_cparams(("arbitrary", "arbitrary", "arbitrary")),
        name="diff_attention",
    )(lam_vec, subln_g, qt, k, vt)


def _route_top2(f, rw_ref, tri_ref, cnt_ref):
    logits = jnp.dot(f.astype(BF16), rw_ref[...], preferred_element_type=F32)
    lane = lax.broadcasted_iota(jnp.int32, logits.shape, 1).astype(F32)
    neg = jnp.full(logits.shape, -jnp.inf, F32)
    lg = jnp.where(lane < N_EXPERTS, logits, neg)
    m1 = jnp.max(lg, axis=1, keepdims=True)
    i1 = jnp.min(jnp.where(lg == m1, lane, float(LANES)), axis=1, keepdims=True)
    lg2 = jnp.where(lane == i1, neg, lg)
    m2 = jnp.max(lg2, axis=1, keepdims=True)
    i2 = jnp.min(jnp.where(lg2 == m2, lane, float(LANES)), axis=1, keepdims=True)
    e = jnp.exp(m2 - m1)
    w1 = 1.0 / (1.0 + e)
    w2 = e / (1.0 + e)
    oh1 = (lane == i1).astype(F32)
    oh2 = (lane == i2).astype(F32)
    cnt = oh1 + oh2
    before = jnp.dot(tri_ref[...], cnt.astype(BF16), preferred_element_type=F32) + cnt_ref[0:1, :]
    r1 = jnp.sum(oh1 * before, axis=1, keepdims=True)
    r2 = jnp.sum(oh2 * before, axis=1, keepdims=True)
    cnt_ref[...] = cnt_ref[...] + jnp.sum(cnt, axis=0, keepdims=True)
    meta = jnp.zeros(logits.shape, F32)
    for j, val in enumerate((i1, i2, w1, w2, r1, r2)):
        meta = jnp.where(lane == float(j), val, meta)
    return meta


def _tail(a, s_ref, w_ref, g1_ref, ng_ref, sh_ref, sc_ref, snew_ref, f_ref):
    y = jnp.dot(a, w_ref[...], preferred_element_type=F32)
    s_new = s_ref[0] + g1_ref[0] * y
    snew_ref[0] = s_new
    f = _modulate(s_new, ng_ref[...], sh_ref[0], sc_ref[0])
    f_ref[0] = f.astype(f_ref.dtype)
    return f


def _wo_kernel(oc_ref, ol_ref, s_ref, w_ref, g1_ref, ng_ref, sh_ref, sc_ref, snew_ref, f_ref):
    is_ctx = pl.program_id(1) == 0
    a = jnp.concatenate(
        [jnp.where(is_ctx, oc_ref[0, h], ol_ref[0, h]) for h in range(N_HEADS)], axis=1)
    _tail(a, s_ref, w_ref, g1_ref, ng_ref, sh_ref, sc_ref, snew_ref, f_ref)


def _wo(o_ctx, o_lat, s, w, g1, ng, sh, sc):
    b, ntok, d = s.shape
    nseg = ntok // SEG
    mod_spec = pl.BlockSpec((1, 1, d), lambda bb, i: (_mod_row(bb, i), 0, 0))
    seg_spec = pl.BlockSpec((1, SEG, d), lambda bb, i: (bb, i, 0))
    return pl.pallas_call(
        _wo_kernel,
        grid=(b, nseg),
        in_specs=[
            pl.BlockSpec((1, N_HEADS, SEG, HEAD_W), lambda bb, i: (bb, 0, 0, 0)),
            pl.BlockSpec((1, N_HEADS, SEG, HEAD_W), lambda bb, i: (bb, 0, jnp.maximum(i - 1, 0), 0)),
            seg_spec,
            pl.BlockSpec((d, d), lambda bb, i: (0, 0)),
            mod_spec,
            pl.BlockSpec((1, d), lambda bb, i: (0, 0)),
            mod_spec, mod_spec,
        ],
        out_specs=[seg_spec, seg_spec],
        out_shape=[jax.ShapeDtypeStruct(s.shape, F32), jax.ShapeDtypeStruct(s.shape, BF16)],
        compiler_params=_cparams(("arbitrary", "arbitrary")),
        name="attn_out_proj",
    )(o_ctx, o_lat, s, w, g1, ng, sh, sc)


def _wout_kernel(h0_ref, h1_ref, gate_ref, s_ref, w_ref, g1_ref, ng_ref, sh_ref, sc_ref,
                 rw_ref, tri_ref, snew_ref, f_ref, meta_ref, cnt_ref):
    @pl.when((pl.program_id(0) == 0) & (pl.program_id(1) == 0))
    def _():
        cnt_ref[...] = jnp.zeros(cnt_ref.shape, F32)

    a = ((h0_ref[0, 0] + h1_ref[0, 0]) * gate_ref[0]).astype(BF16)
    f = _tail(a, s_ref, w_ref, g1_ref, ng_ref, sh_ref, sc_ref, snew_ref, f_ref)
    meta_ref[0] = _route_top2(f, rw_ref, tri_ref, cnt_ref)


def _wout(hs, gate, s, w, g1, ng, sh, sc, rw, tri):
    b, ntok, d = s.shape
    nseg = ntok // SEG
    mod_spec = pl.BlockSpec((1, 1, d), lambda bb, i: (_mod_row(bb, i), 0, 0))
    seg_spec = pl.BlockSpec((1, SEG, d), lambda bb, i: (bb, i, 0))
    return pl.pallas_call(
        _wout_kernel,
        grid=(b, nseg),
        in_specs=[
            pl.BlockSpec((1, 1, SEG, d), lambda bb, i: (0, bb, i, 0)),
            pl.BlockSpec((1, 1, SEG, d), lambda bb, i: (1, bb, i, 0)),
            seg_spec, seg_spec,
            pl.BlockSpec((d, d), lambda bb, i: (0, 0)),
            mod_spec,
            pl.BlockSpec((1, d), lambda bb, i: (0, 0)),
            mod_spec, mod_spec,
            pl.BlockSpec((d, LANES), lambda bb, i: (0, 0)),
            pl.BlockSpec((SEG, SEG), lambda bb, i: (0, 0)),
        ],
        out_specs=[
            seg_spec, seg_spec,
            pl.BlockSpec((1, SEG, LANES), lambda bb, i: (bb, i, 0)),
            pl.BlockSpec((8, LANES), lambda bb, i: (0, 0)),
        ],
        out_shape=[
            jax.ShapeDtypeStruct(s.shape, F32),
            jax.ShapeDtypeStruct(s.shape, F32),
            jax.ShapeDtypeStruct((b, ntok, LANES), F32),
            jax.ShapeDtypeStruct((8, LANES), F32),
        ],
        compiler_params=_cparams(("arbitrary", "arbitrary")),
        name="lru_out_proj_router",
    )(hs, hs, gate, s, w, g1, ng, sh, sc, rw, tri)


def _swiglu_acc(x, wg_ref, wu_ref, wd_ref, acc_ref, j):
    @pl.when(j == 0)
    def _():
        acc_ref[...] = jnp.zeros(acc_ref.shape, F32)

    hg = jnp.dot(x, wg_ref[0], preferred_element_type=F32)
    hu = jnp.dot(x, wu_ref[0], preferred_element_type=F32)
    a = ((hg * jax.nn.sigmoid(hg)) * hu).astype(BF16)
    acc_ref[...] += jnp.dot(a, wd_ref[0], preferred_element_type=F32)


def _ffn_dense_kernel(f_ref, wg_ref, wu_ref, wd_ref, s_ref, gl_ref, gc_ref, o_ref, acc_ref):
    j = pl.program_id(2)
    _swiglu_acc(f_ref[0], wg_ref, wu_ref, wd_ref, acc_ref, j)

    @pl.when(j == pl.num_programs(2) - 1)
    def _():
        rows = lax.broadcasted_iota(jnp.int32, acc_ref.shape, 0)
        is_ctx = (rows < SEG) & (pl.program_id(1) == 0)
        gate = jnp.where(is_ctx, gc_ref[0], gl_ref[0])
        o_ref[0] = s_ref[0] + gate * acc_ref[...]


def _ffn_dense(f, wgu, wd, s, g2):
    b, ntok, d = s.shape
    ff = wd.shape[1]
    nf = ff // FFN_TF
    row_spec = pl.BlockSpec((1, FFN_TM, d), lambda bb, i, j: (bb, i, 0))
    return pl.pallas_call(
        _ffn_dense_kernel,
        grid=(b, ntok // FFN_TM, nf),
        in_specs=[
            row_spec,
            pl.BlockSpec((1, d, FFN_TF), lambda bb, i, j: (0, 0, j)),
            pl.BlockSpec((1, d, FFN_TF), lambda bb, i, j: (0, 0, j + nf)),
            pl.BlockSpec((1, FFN_TF, d), lambda bb, i, j: (0, j, 0)),
            row_spec,
            pl.BlockSpec((1, 1, d), lambda bb, i, j: (bb, 0, 0)),
            pl.BlockSpec((1, 1, d), lambda bb, i, j: (4, 0, 0)),
        ],
        out_specs=row_spec,
        out_shape=jax.ShapeDtypeStruct(s.shape, F32),
        scratch_shapes=[pltpu.VMEM((FFN_TM, d), F32)],
        compiler_params=_cparams(("arbitrary", "arbitrary", "arbitrary")),
        name="ffn_dense",
    )(f, wgu, wgu, wd, s, g2, g2)


def _ffn_moe_kernel(te_ref, nt_ref, x_ref, wg_ref, wu_ref, wd_ref, o_ref, acc_ref):
    i = pl.program_id(0)
    j = pl.program_id(1)

    @pl.when(i < nt_ref[0])
    def _():
        _swiglu_acc(x_ref[...].astype(BF16), wg_ref, wu_ref, wd_ref, acc_ref, j)

    @pl.when(j == pl.num_programs(1) - 1)
    def _():
        o_ref[...] = acc_ref[...]


def _ffn_moe(tile_expert, n_tiles, xs, wgu, wd):
    p, d = xs.shape
    ff = wd.shape[1]
    nf = ff // MOE_TF
    row_spec = pl.BlockSpec((MOE_TM, d), lambda i, j, te, nt: (i, 0))
    grid_spec = pltpu.PrefetchScalarGridSpec(
        num_scalar_prefetch=2,
        grid=(p // MOE_TM, nf),
        in_specs=[
            row_spec,
            pl.BlockSpec((1, d, MOE_TF), lambda i, j, te, nt: (te[i], 0, j)),
            pl.BlockSpec((1, d, MOE_TF), lambda i, j, te, nt: (te[i], 0, j + nf)),
            pl.BlockSpec((1, MOE_TF, d), lambda i, j, te, nt: (te[i], j, 0)),
        ],
        out_specs=row_spec,
        scratch_shapes=[pltpu.VMEM((MOE_TM, d), F32)],
    )
    return pl.pallas_call(
        _ffn_moe_kernel,
        grid_spec=grid_spec,
        out_shape=jax.ShapeDtypeStruct((p, d), F32),
        compiler_params=_cparams(("arbitrary", "arbitrary")),
        name="ffn_experts",
    )(tile_expert, n_tiles, xs, wgu, wgu, wd)


def _row_copy(src_ref, src_row, dst_ref, dst_row, sem):
    return pltpu.make_async_copy(src_ref.at[pl.ds(src_row, 1), :], dst_ref.at[pl.ds(dst_row, 1), :], sem)


def _dispatch_kernel(pos_ref, f_ref, xs_in_ref, xs_ref, pos_smem, sem_pos, sem_rows):
    del xs_in_ref
    cp = pltpu.make_async_copy(pos_ref.at[0, 0], pos_smem, sem_pos)
    cp.start()
    cp.wait()

    def issue(r, c):
        _row_copy(f_ref, r, xs_ref, pos_smem[2 * r], sem_rows).start(priority=0)
        _row_copy(f_ref, r, xs_ref, pos_smem[2 * r + 1], sem_rows).start(priority=1)
        return c

    lax.fori_loop(0, SEG, issue, 0)

    def drain(r, c):
        _row_copy(f_ref, 0, xs_ref, 0, sem_rows).wait()
        _row_copy(f_ref, 0, xs_ref, 0, sem_rows).wait()
        return c

    lax.fori_loop(0, SEG, drain, 0)


def _dispatch(pos, f_rows, xs_init):
    t, d = f_rows.shape
    return pl.pallas_call(
        _dispatch_kernel,
        grid=(t // SEG,),
        in_specs=[
            pl.BlockSpec((1, 1, 2 * SEG), lambda i: (i, 0, 0)),
            pl.BlockSpec((SEG, d), lambda i: (i, 0)),
            pl.BlockSpec(memory_space=pl.ANY),
        ],
        out_specs=pl.BlockSpec(memory_space=pl.ANY),
        out_shape=jax.ShapeDtypeStruct(xs_init.shape, F32),
        scratch_shapes=[
            pltpu.SMEM((2 * SEG,), jnp.int32),
            pltpu.SemaphoreType.DMA,
            pltpu.SemaphoreType.DMA,
        ],
        input_output_aliases={2: 0},
        compiler_params=_cparams(("arbitrary",)),
        name="moe_dispatch",
    )(pos, f_rows, xs_init)


def _combine_kernel(pos_ref, meta_ref, s_ref, g2_ref, fg_ref, ys_ref, o_ref,
                    pos_smem, buf_ref, sem_pos, sem_rows, *, final):
    cp = pltpu.make_async_copy(pos_ref.at[0, 0], pos_smem, sem_pos)
    cp.start()
    cp.wait()

    def issue(r, c):
        _row_copy(ys_ref, pos_smem[2 * r], buf_ref.at[0], r, sem_rows).start(priority=0)
        _row_copy(ys_ref, pos_smem[2 * r + 1], buf_ref.at[1], r, sem_rows).start(priority=1)
        return c

    lax.fori_loop(0, SEG, issue, 0)

    def drain(r, c):
        _row_copy(ys_ref, 0, buf_ref.at[0], 0, sem_rows).wait()
        _row_copy(ys_ref, 0, buf_ref.at[1], 0, sem_rows).wait()
        return c

    lax.fori_loop(0, SEG, drain, 0)

    meta = meta_ref[0]
    y = meta[:, 2:3] * buf_ref[0] + meta[:, 3:4] * buf_ref[1]
    s_new = s_ref[0] + g2_ref[0] * y
    if final:
        s_new = _rms(s_new) * fg_ref[...]
    o_ref[0] = s_new


def _combine(pos, meta, s, g2, final_g, ys, *, final):
    b, ntok, d = s.shape
    nseg = ntok // SEG
    off = 1 if final else 0
    nout = nseg - off
    kern = functools.partial(_combine_kernel, final=final)
    return pl.pallas_call(
        kern,
        grid=(b, nout),
        in_specs=[
            pl.BlockSpec((1, 1, 2 * SEG), lambda bb, i: (bb * nseg + i + off, 0, 0)),
            pl.BlockSpec((1, SEG, LANES), lambda bb, i: (bb, i + off, 0)),
            pl.BlockSpec((1, SEG, d), lambda bb, i: (bb, i + off, 0)),
            pl.BlockSpec((1, 1, d), lambda bb, i: (_mod_row(bb, i + off), 0, 0)),
            pl.BlockSpec((1, d), lambda bb, i: (0, 0)),
            pl.BlockSpec(memory_space=pl.ANY),
        ],
        out_specs=pl.BlockSpec((1, SEG, d), lambda bb, i: (bb, i, 0)),
        out_shape=jax.ShapeDtypeStruct((b, nout * SEG, d), F32),
        scratch_shapes=[
            pltpu.SMEM((2 * SEG,), jnp.int32),
            pltpu.VMEM((2, SEG, d), F32),
            pltpu.SemaphoreType.DMA,
            pltpu.SemaphoreType.DMA,
        ],
        compiler_params=_cparams(("arbitrary", "arbitrary")),
        name="moe_combine",
    )(pos, meta, s, g2, final_g, ys)


def _win_kernel(s_ref, g_ref, sh_ref, sc_ref, w_ref, gate_ref, u_ref):
    d = s_ref.shape[-1]
    h = _modulate(s_ref[0], g_ref[...], sh_ref[0], sc_ref[0]).astype(BF16)
    gate_ref[0] = jax.nn.gelu(jnp.dot(h, w_ref[:, :d], preferred_element_type=F32))
    u_ref[0] = jnp.dot(h, w_ref[:, d:], preferred_element_type=F32)


def _win(s, g, sh, sc, w):
    b, ntok, d = s.shape
    mod_spec = pl.BlockSpec((1, 1, d), lambda bb, i: (_mod_row(bb, i), 0, 0))
    seg_spec = pl.BlockSpec((1, SEG, d), lambda bb, i: (bb, i, 0))
    return pl.pallas_call(
        _win_kernel,
        grid=(b, ntok // SEG),
        in_specs=[
            seg_spec,
            pl.BlockSpec((1, d), lambda bb, i: (0, 0)),
            mod_spec, mod_spec,
            pl.BlockSpec((d, 2 * d), lambda bb, i: (0, 0)),
        ],
        out_specs=[seg_spec, seg_spec],
        out_shape=[jax.ShapeDtypeStruct(s.shape, F32), jax.ShapeDtypeStruct(s.shape, F32)],
        compiler_params=_cparams(("arbitrary", "arbitrary")),
        name="lru_in_proj",
    )(s, g, sh, sc, w)


def _lru_block(direction, step, nseg):
    return jnp.where((direction == 0) | (step == 0), step, nseg - step)


def _lru_kernel(u_ref, up_ref, un_ref, cw_ref, cb_ref, gw_ref, gb_ref, ap_ref, h_ref,
                a_ref, b_ref, st_ref, *, nseg):
    direction = pl.program_id(0)
    step = pl.program_id(2)
    blk = _lru_block(direction, step, nseg)
    d = u_ref.shape[-1]

    u = u_ref[0]
    has_prev = blk > 1
    has_next = (blk > 0) & (blk < nseg - 1)
    zero_row = jnp.zeros((1, d), F32)
    prev2 = jnp.where(has_prev, up_ref[0, 6:7, :], zero_row)
    prev1 = jnp.where(has_prev, up_ref[0, 7:8, :], zero_row)
    next1 = jnp.where(has_next, un_ref[0, 0:1, :], zero_row)
    rows = lax.broadcasted_iota(jnp.int32, u.shape, 0)
    um1 = jnp.where(rows == 0, prev1, pltpu.roll(u, 1, 0))
    um2 = jnp.where(rows == 0, prev2, jnp.where(rows == 1, prev1, pltpu.roll(u, 2, 0)))
    up1 = jnp.where(rows == SEG - 1, next1, pltpu.roll(u, SEG - 1, 0))
    cw = cw_ref[...]
    uc = cw[0:1] * um2 + cw[1:2] * um1 + cw[2:3] * u + cw[3:4] * up1 + cb_ref[...]

    ub = uc.astype(BF16)
    blkw = d // N_LRU_BLOCKS
    gr, gi = [], []
    for kb in range(N_LRU_BLOCKS):
        g = jnp.dot(ub[:, kb * blkw:(kb + 1) * blkw], gw_ref[0, kb], preferred_element_type=F32)
        gr.append(g[:, :blkw])
        gi.append(g[:, blkw:])
    gb = gb_ref[0]
    r_t = jax.nn.sigmoid(jnp.concatenate(gr, axis=1) + gb[0:1])
    i_t = jax.nn.sigmoid(jnp.concatenate(gi, axis=1) + gb[1:2])
    ap = ap_ref[0]
    log_sig = jnp.minimum(ap, 0.0) - jnp.log1p(jnp.exp(-jnp.abs(ap)))
    log_a = RG_C * r_t * log_sig
    a_t = jnp.exp(log_a)
    a_ref[...] = a_t
    b_ref[...] = jnp.sqrt(1.0 - a_t * a_t) * (i_t * uc)

    @pl.when(step == 0)
    def _():
        st_ref[...] = jnp.zeros(st_ref.shape, F32)

    def run(reverse):
        def body(t, hprev):
            tt = SEG - 1 - t if reverse else t
            hnew = a_ref[pl.ds(tt, 1), :] * hprev + b_ref[pl.ds(tt, 1), :]
            h_ref[0, 0, pl.ds(tt, 1), :] = hnew
            return hnew
        st_ref[...] = lax.fori_loop(0, SEG, body, st_ref[...], unroll=8)

    @pl.when(direction == 0)
    def _():
        run(False)

    @pl.when(direction == 1)
    def _():
        run(True)


def _lru(u, conv_w, conv_b, gate_w, gate_b, a_param):
    b, ntok, d = u.shape
    nseg = ntok // SEG
    sub = SEG // 8
    nsub = ntok // 8
    blkw = d // N_LRU_BLOCKS

    def blk(dd, i):
        return _lru_block(dd, i, nseg)

    kern = functools.partial(_lru_kernel, nseg=nseg)
    return pl.pallas_call(
        kern,
        grid=(2, b, nseg),
        in_specs=[
            pl.BlockSpec((1, SEG, d), lambda dd, bb, i: (bb, blk(dd, i), 0)),
            pl.BlockSpec((1, 8, d), lambda dd, bb, i: (bb, jnp.maximum(blk(dd, i) * sub - 1, 0), 0)),
            pl.BlockSpec((1, 8, d), lambda dd, bb, i: (bb, jnp.minimum((blk(dd, i) + 1) * sub, nsub - 1), 0)),
            pl.BlockSpec((4, d), lambda dd, bb, i: (0, 0)),
            pl.BlockSpec((1, d), lambda dd, bb, i: (0, 0)),
            pl.BlockSpec((1, N_LRU_BLOCKS, blkw, 2 * blkw), lambda dd, bb, i: (dd, 0, 0, 0)),
            pl.BlockSpec((1, 2, d), lambda dd, bb, i: (dd, 0, 0)),
            pl.BlockSpec((1, 1, d), lambda dd, bb, i: (dd, 0, 0)),
        ],
        out_specs=pl.BlockSpec((1, 1, SEG, d), lambda dd, bb, i: (dd, bb, blk(dd, i), 0)),
        out_shape=jax.ShapeDtypeStruct((2, b, ntok, d), F32),
        scratch_shapes=[
            pltpu.VMEM((SEG, d), F32),
            pltpu.VMEM((SEG, d), F32),
            pltpu.VMEM((1, d), F32),
        ],
        compiler_params=_cparams(("arbitrary", "arbitrary", "arbitrary")),
        name="rglru_scan",
    )(u, u, u, conv_w, conv_b, gate_w, gate_b, a_param)


def _rope_tables(n_ctx, n_lat):
    rows = n_lat // GRID_W
    row = jnp.repeat(jnp.arange(rows, dtype=F32), GRID_W)
    col = jnp.tile(jnp.arange(GRID_W, dtype=F32), rows)
    inv_freq = 1.0 / (ROPE_BASE ** (jnp.arange(0, ROPE_AXIS_DIM, 2, dtype=F32) / ROPE_AXIS_DIM))
    ang = jnp.stack([row[:, None] * inv_freq, col[:, None] * inv_freq], axis=1)
    ang = jnp.broadcast_to(ang[:, :, None, :], (n_lat, 2, 2, ROPE_AXIS_DIM // 2)).reshape(n_lat, HEAD_DIM)
    ang = jnp.concatenate([jnp.zeros((n_ctx, HEAD_DIM), F32), ang], axis=0)
    cos = jnp.cos(ang)
    sin = jnp.sin(ang)
    lane = jnp.arange(HEAD_DIM)
    sin = jnp.where((lane % ROPE_AXIS_DIM) < (ROPE_AXIS_DIM // 2), -sin, sin)
    return jnp.tile(cos, (1, 2)), jnp.tile(sin, (1, 2))


def kernel(x, c, ctx, c_ctx, mod_w, mod_b, norm_mix_g, norm_ffn_g, attn_w_qkv, attn_w_o, attn_lambda, attn_subln_g, lru_w_in, lru_conv_w, lru_conv_b, lru_gate_w, lru_gate_b, lru_a_param, lru_w_out, ffn_w_gate_up, ffn_w_down, moe_router_w, moe_w_gate_up, moe_w_down, final_norm_g):
    bsz, n_lat, d = x.shape
    n_ctx = ctx.shape[1]
    depth = mod_w.shape[0]
    assert n_ctx == SEG and n_lat % ATT_TQ == 0 and bsz <= 4
    ntok = n_ctx + n_lat
    assert ntok % ATT_TK == 0 and ntok % FFN_TM == 0
    nseg = ntok // SEG
    t_all = bsz * ntok

    s = jnp.concatenate([ctx, x], axis=1)
    c_pad = jnp.zeros((8, d), F32).at[:bsz].set(c).at[4].set(c_ctx)
    mod = _modulation(c_pad, mod_w, mod_b)
    cos, sin = _rope_tables(n_ctx, n_lat)
    tri = (jnp.arange(SEG)[:, None] > jnp.arange(SEG)[None, :]).astype(BF16)
    blkw = d // N_LRU_BLOCKS
    out = None

    for i in range(depth):
        j = i // 2
        last = i == depth - 1
        sh1, sc1, g1, sh2, sc2, g2 = [mod[i, :, k * d:(k + 1) * d].reshape(8, 1, d) for k in range(6)]
        ng_mix = norm_mix_g[i].reshape(1, d)
        ng_ffn = norm_ffn_g[i].reshape(1, d)
        if i % 2 == 0:
            lambda_init = 0.8 - 0.6 * math.exp(-0.3 * i)
            qt, k, vt = _qkv(s, ng_mix, sh1, sc1, attn_w_qkv[j].astype(BF16), cos, sin)
            subln = attn_subln_g[j].reshape(1, HEAD_W)
            o_ctx = _attention(attn_lambda[j], subln, qt, k, vt, q_off=0, nq=n_ctx,
                               n_chunks=1, tk=SEG, lambda_init=lambda_init)
            o_lat = _attention(attn_lambda[j], subln, qt, k, vt, q_off=n_ctx, nq=n_lat,
                               n_chunks=ntok // ATT_TK, tk=ATT_TK, lambda_init=lambda_init)
            s, f = _wo(o_ctx, o_lat, s, attn_w_o[j].astype(BF16), g1, ng_ffn, sh2, sc2)
            s = _ffn_dense(f, ffn_w_gate_up[j].astype(BF16)[None], ffn_w_down[j].astype(BF16)[None], s, g2)
        else:
            gate, u = _win(s, ng_mix, sh1, sc1, lru_w_in[j].astype(BF16))
            gw = lru_gate_w[j].transpose(0, 2, 3, 1, 4).reshape(2, N_LRU_BLOCKS, blkw, 2 * blkw).astype(BF16)
            hs = _lru(u, lru_conv_w[j], lru_conv_b[j].reshape(1, d), gw, lru_gate_b[j],
                      lru_a_param[j].reshape(2, 1, d))
            rw = jnp.zeros((d, LANES), BF16).at[:, :N_EXPERTS].set(moe_router_w[j].astype(BF16))
            s, f, meta, cnt = _wout(hs, gate, s, lru_w_out[j].astype(BF16), g1, ng_ffn, sh2, sc2, rw, tri)

            meta2 = meta.reshape(t_all, LANES)
            idx = meta2[:, 0:2].astype(jnp.int32)
            rank = meta2[:, 4:6].astype(jnp.int32)
            counts = cnt[0, :N_EXPERTS].astype(jnp.int32)
            padded = ((counts + MOE_TM - 1) // MOE_TM) * MOE_TM
            ends = jnp.cumsum(padded)
            starts = ends - padded
            pos = (starts[idx] + rank).reshape(t_all // SEG, 1, 2 * SEG)
            p_rows = 2 * t_all + N_EXPERTS * MOE_TM
            tile_start = jnp.arange(p_rows // MOE_TM, dtype=jnp.int32) * MOE_TM
            tile_expert = jnp.minimum(jnp.sum((tile_start[:, None] >= ends[None, :]).astype(jnp.int32), axis=1),
                                      N_EXPERTS - 1)
            n_tiles = (ends[-1:] // MOE_TM).astype(jnp.int32)

            xs = _dispatch(pos, f.reshape(t_all, d), jnp.zeros((p_rows, d), F32))
            ys = _ffn_moe(tile_expert, n_tiles, xs, moe_w_gate_up[j].astype(BF16), moe_w_down[j].astype(BF16))
            res = _combine(pos, meta, s, g2, final_norm_g.reshape(1, d), ys, final=last)
            if last:
                out = res
            else:
                s = res
    return out
```

```python
import functools
import math

import jax
import jax.numpy as jnp
from jax import lax
from jax.experimental import pallas as pl
from jax.experimental.pallas import tpu as pltpu

F32 = jnp.float32
BF16 = jnp.bfloat16

N_HEADS = 8
HEAD_DIM = 64
HEAD_W = 2 * HEAD_DIM
GRID_W = 64
ROPE_BASE = 10000.0
ROPE_AXIS_DIM = HEAD_DIM // 2
N_LRU_BLOCKS = 8
RG_C = 8.0
N_EXPERTS = 8
EPS = 1e-6
QK_SCALE = HEAD_DIM ** -0.5 * math.log2(math.e)

LANES = 128
SEG = 256
ATT_TQ = 256
ATT_TK = 768
FFN_TM = 768
FFN_TF = 1408
MOE_TM = 512
MOE_TF = 1792
MOD_TN = 1536
VMEM_LIMIT = 56 * 1024 * 1024


def _cparams(sem):
    return pltpu.CompilerParams(dimension_semantics=sem, vmem_limit_bytes=VMEM_LIMIT)


def _rms(x):
    return x * lax.rsqrt(jnp.mean(x * x, axis=-1, keepdims=True) + EPS)


def _modulate(x, g, shift, scale):
    return (_rms(x) * g) * (1.0 + scale) + shift


def _pack_pairs(x):
    bits = pltpu.bitcast(x.astype(BF16).astype(F32), jnp.uint32)
    c = x.shape[1] // 2
    return (bits[:, :c] >> 16) | bits[:, c:]


def _unpack_pairs(u):
    lo = pltpu.bitcast(u << 16, F32)
    hi = pltpu.bitcast(u & jnp.uint32(0xFFFF0000), F32)
    return jnp.concatenate([lo, hi], axis=1)


def _mod_row(b, i):
    return jnp.where(i == 0, 4, b)


def _mod_kernel(c_ref, w_ref, b_ref, o_ref):
    c = c_ref[...]
    a = (c * jax.nn.sigmoid(c)).astype(BF16)
    o_ref[0] = jnp.dot(a, w_ref[0].astype(BF16), preferred_element_type=F32) + b_ref[0]


def _modulation(c_pad, mod_w, mod_b):
    depth, d, n = mod_w.shape
    return pl.pallas_call(
        _mod_kernel,
        grid=(depth, n // MOD_TN),
        in_specs=[
            pl.BlockSpec((8, d), lambda l, j: (0, 0)),
            pl.BlockSpec((1, d, MOD_TN), lambda l, j: (l, 0, j)),
            pl.BlockSpec((1, 1, MOD_TN), lambda l, j: (l, 0, j)),
        ],
        out_specs=pl.BlockSpec((1, 8, MOD_TN), lambda l, j: (l, 0, j)),
        out_shape=jax.ShapeDtypeStruct((depth, 8, n), F32),
        compiler_params=_cparams(("arbitrary", "arbitrary")),
        name="modulation",
    )(c_pad, mod_w, mod_b.reshape(depth, 1, n))


def _qkv_kernel(s_ref, g_ref, sh_ref, sc_ref, w_ref, cos_ref, sin_ref, qt_ref, k_ref, vt_ref):
    d = s_ref.shape[-1]
    h = _modulate(s_ref[0], g_ref[...], sh_ref[0], sc_ref[0]).astype(BF16)
    cos = cos_ref[...]
    sin = sin_ref[...]
    lane = lax.broadcasted_iota(jnp.int32, cos.shape, 1)
    first = (lane % ROPE_AXIS_DIM) < (ROPE_AXIS_DIM // 2)

    def rope(x):
        rot = jnp.where(first, pltpu.roll(x, LANES - ROPE_AXIS_DIM // 2, 1),
                        pltpu.roll(x, ROPE_AXIS_DIM // 2, 1))
        return x * cos + rot * sin

    pair = 2 * HEAD_W
    for c in range(N_HEADS // 2):
        rq = jnp.dot(h, w_ref[:, c * pair:(c + 1) * pair], preferred_element_type=F32)
        rk = jnp.dot(h, w_ref[:, d + c * pair:d + (c + 1) * pair], preferred_element_type=F32)
        rv = jnp.dot(h, w_ref[:, 2 * d + c * pair:2 * d + (c + 1) * pair], preferred_element_type=F32)
        for u in range(2):
            hh = 2 * c + u
            sl = slice(u * HEAD_W, (u + 1) * HEAD_W)
            q = rope(rq[:, sl]) * QK_SCALE
            qt_ref[0, hh, 0] = q.T.astype(BF16)
            k_ref[0, hh] = rope(rk[:, sl]).astype(BF16)
            vt_ref[0, hh, 0] = rv[:, sl].T.astype(BF16)


def _qkv(s, g, sh, sc, w, cos, sin):
    b, ntok, d = s.shape
    nseg = ntok // SEG
    per = ATT_TK // SEG
    mod_spec = pl.BlockSpec((1, 1, d), lambda bb, i: (_mod_row(bb, i), 0, 0))
    return pl.pallas_call(
        _qkv_kernel,
        grid=(b, nseg),
        in_specs=[
            pl.BlockSpec((1, SEG, d), lambda bb, i: (bb, i, 0)),
            pl.BlockSpec((1, d), lambda bb, i: (0, 0)),
            mod_spec, mod_spec,
            pl.BlockSpec((d, 3 * d), lambda bb, i: (0, 0)),
            pl.BlockSpec((SEG, HEAD_W), lambda bb, i: (i, 0)),
            pl.BlockSpec((SEG, HEAD_W), lambda bb, i: (i, 0)),
        ],
        out_specs=[
            pl.BlockSpec((1, N_HEADS, 1, HEAD_W, SEG), lambda bb, i: (bb, 0, i, 0, 0)),
            pl.BlockSpec((1, N_HEADS, SEG, HEAD_W), lambda bb, i: (bb, 0, i, 0)),
            pl.BlockSpec((1, N_HEADS, 1, HEAD_W, SEG), lambda bb, i: (bb, 0, i // per, 0, i % per)),
        ],
        out_shape=[
            jax.ShapeDtypeStruct((b, N_HEADS, nseg, HEAD_W, SEG), BF16),
            jax.ShapeDtypeStruct((b, N_HEADS, ntok, HEAD_W), BF16),
            jax.ShapeDtypeStruct((b, N_HEADS, ntok // ATT_TK, HEAD_W, ATT_TK), BF16),
        ],
        compiler_params=_cparams(("arbitrary", "arbitrary")),
        name="qkv_rope",
    )(s, g, sh, sc, w, cos, sin)


def _attn_kernel(lam_ref, g_ref, qt_ref, k_ref, vt_ref, o_ref, sa_ref, sb_ref, l_ref, acc_ref,
                 *, n_qt, q_blk, n_chunks, tk, lambda_init):
    n_steps = n_qt * n_chunks
    row = lax.broadcasted_iota(jnp.int32, (HEAD_W, ATT_TQ), 0)
    zero = jnp.zeros((HEAD_W, ATT_TQ), BF16)
    neg_inf = jnp.full((1, ATT_TQ), -jnp.inf, F32)

    lv = lam_ref[...]
    lam = (jnp.exp(jnp.sum(lv[0:1] * lv[1:2], axis=1, keepdims=True))
           - jnp.exp(jnp.sum(lv[2:3] * lv[3:4], axis=1, keepdims=True)) + lambda_init)

    l_ref[...] = jnp.zeros(l_ref.shape, F32)
    acc_ref[...] = jnp.zeros(acc_ref.shape, F32)

    def stage_a(t, s_ref, m_run):
        t = jnp.minimum(t, n_steps - 1)
        qi = t // n_chunks
        c = t - qi * n_chunks
        qt = qt_ref[0, 0, qi + q_blk]
        qparts = (jnp.where(row < HEAD_DIM, qt, zero), jnp.where(row >= HEAD_DIM, qt, zero))
        kc = k_ref[0, 0, pl.ds(pl.multiple_of(c * tk, tk), tk), :]
        m_new, alpha = [], []
        for p in range(2):
            s = jnp.dot(kc, qparts[p], preferred_element_type=F32)
            s_ref[p] = s
            m_in = jnp.where(c == 0, neg_inf, m_run[p])
            mx = jnp.maximum(m_in, jnp.max(s, axis=0, keepdims=True))
            alpha.append(jnp.exp2(m_in - mx))
            m_new.append(mx)
        return tuple(m_new), tuple(alpha)

    def stage_b(t, s_ref, m_run, alpha):
        qi = t // n_chunks
        c = t - qi * n_chunks
        vc = vt_ref[0, 0, c]
        for p in range(2):
            pr = jnp.exp2(s_ref[p] - m_run[p])
            l_ref[p] = alpha[p] * l_ref[p] + jnp.sum(pr, axis=0, keepdims=True)
            acc_ref[p] = alpha[p] * acc_ref[p] + jnp.dot(vc, pr.astype(BF16), preferred_element_type=F32)

        @pl.when(c == n_chunks - 1)
        def _():
            ot = acc_ref[0] / l_ref[0] - lam * (acc_ref[1] / l_ref[1])
            ot = ot * lax.rsqrt(jnp.mean(ot * ot, axis=0, keepdims=True) + EPS)
            o = ((ot.T * g_ref[...]) * (1.0 - lambda_init)).astype(o_ref.dtype)
            o_ref[0, 0, pl.ds(pl.multiple_of(qi * ATT_TQ, ATT_TQ), ATT_TQ), :] = o

    m_run, alpha = stage_a(0, sa_ref, (neg_inf, neg_inf))

    def pair(i, carry):
        m_run, alpha = carry
        m_nxt, a_nxt = stage_a(2 * i + 1, sb_ref, m_run)
        stage_b(2 * i, sa_ref, m_run, alpha)
        m_run, alpha = stage_a(2 * i + 2, sa_ref, m_nxt)
        stage_b(2 * i + 1, sb_ref, m_nxt, a_nxt)
        return m_run, alpha

    if n_steps // 2:
        m_run, alpha = lax.fori_loop(0, n_steps // 2, pair, (m_run, alpha))
    if n_steps % 2:
        stage_b(jnp.int32(n_steps - 1), sa_ref, m_run, alpha)


def _attention(lam_vec, subln_g, qt, k, vt, *, q_off, nq, n_chunks, tk, lambda_init):
    b, nh, nqt_all, _, _ = qt.shape
    kern = functools.partial(_attn_kernel, n_qt=nq // ATT_TQ, q_blk=q_off // ATT_TQ,
                             n_chunks=n_chunks, tk=tk, lambda_init=lambda_init)
    vt_block = (1, 1, n_chunks, HEAD_W, tk)
    return pl.pallas_call(
        kern,
        grid=(b, nh),
        in_specs=[
            pl.BlockSpec((4, HEAD_DIM), lambda bb, h: (0, 0)),
            pl.BlockSpec((1, HEAD_W), lambda bb, h: (0, 0)),
            pl.BlockSpec((1, 1, nqt_all, HEAD_W, ATT_TQ), lambda bb, h: (bb, h, 0, 0, 0)),
            pl.BlockSpec((1, 1, n_chunks * tk, HEAD_W), lambda bb, h: (bb, h, 0, 0)),
            pl.BlockSpec(vt_block, lambda bb, h: (bb, h, 0, 0, 0)),
        ],
        out_specs=pl.BlockSpec((1, 1, nq, HEAD_W), lambda bb, h: (bb, h, 0, 0)),
        out_shape=jax.ShapeDtypeStruct((b, nh, nq, HEAD_W), BF16),
        scratch_shapes=[
            pltpu.VMEM((2, tk, ATT_TQ), F32),
            pltpu.VMEM((2, tk, ATT_TQ), F32),
            pltpu.VMEM((2, 1, ATT_TQ), F32),
            pltpu.VMEM((2, HEAD_W, ATT_TQ), F32),
        ],
        compiler_params=_cparams(("arbitrary", "arbitrary")),
        name="diff_attention",
    )(lam_vec, subln_g, qt, k, vt)


def _route_top2(f, rw_ref, tri_ref, cnt_ref):
    logits = jnp.dot(f.astype(BF16), rw_ref[...], preferred_element_type=F32)
    lane = lax.broadcasted_iota(jnp.int32, logits.shape, 1).astype(F32)
    neg = jnp.full(logits.shape, -jnp.inf, F32)
    lg = jnp.where(lane < N_EXPERTS, logits, neg)
    m1 = jnp.max(lg, axis=1, keepdims=True)
    i1 = jnp.min(jnp.where(lg == m1, lane, float(LANES)), axis=1, keepdims=True)
    lg2 = jnp.where(lane == i1, neg, lg)
    m2 = jnp.max(lg2, axis=1, keepdims=True)
    i2 = jnp.min(jnp.where(lg2 == m2, lane, float(LANES)), axis=1, keepdims=True)
    e = jnp.exp(m2 - m1)
    w1 = 1.0 / (1.0 + e)
    w2 = e / (1.0 + e)
    oh1 = (lane == i1).astype(F32)
    oh2 = (lane == i2).astype(F32)
    cnt = oh1 + oh2
    before = jnp.dot(tri_ref[...], cnt.astype(BF16), preferred_element_type=F32) + cnt_ref[0:1, :]
    r1 = jnp.sum(oh1 * before, axis=1, keepdims=True)
    r2 = jnp.sum(oh2 * before, axis=1, keepdims=True)
    cnt_ref[...] = cnt_ref[...] + jnp.sum(cnt, axis=0, keepdims=True)
    meta = jnp.zeros(logits.shape, F32)
    for j, val in enumerate((i1, i2, w1, w2, r1, r2)):
        meta = jnp.where(lane == float(j), val, meta)
    return meta


def _tail(a, s_ref, w_ref, g1_ref, ng_ref, sh_ref, sc_ref, snew_ref, f_ref):
    y = jnp.dot(a, w_ref[...], preferred_element_type=F32)
    s_new = s_ref[0] + g1_ref[0] * y
    snew_ref[0] = s_new
    f = _modulate(s_new, ng_ref[...], sh_ref[0], sc_ref[0])
    f_ref[0] = _pack_pairs(f) if f_ref.dtype == jnp.uint32 else f.astype(f_ref.dtype)
    return f


def _wo_kernel(oc_ref, ol_ref, s_ref, w_ref, g1_ref, ng_ref, sh_ref, sc_ref, snew_ref, f_ref):
    is_ctx = pl.program_id(1) == 0
    a = jnp.concatenate(
        [jnp.where(is_ctx, oc_ref[0, h], ol_ref[0, h]) for h in range(N_HEADS)], axis=1)
    _tail(a, s_ref, w_ref, g1_ref, ng_ref, sh_ref, sc_ref, snew_ref, f_ref)


def _wo(o_ctx, o_lat, s, w, g1, ng, sh, sc):
    b, ntok, d = s.shape
    nseg = ntok // SEG
    mod_spec = pl.BlockSpec((1, 1, d), lambda bb, i: (_mod_row(bb, i), 0, 0))
    seg_spec = pl.BlockSpec((1, SEG, d), lambda bb, i: (bb, i, 0))
    return pl.pallas_call(
        _wo_kernel,
        grid=(b, nseg),
        in_specs=[
            pl.BlockSpec((1, N_HEADS, SEG, HEAD_W), lambda bb, i: (bb, 0, 0, 0)),
            pl.BlockSpec((1, N_HEADS, SEG, HEAD_W), lambda bb, i: (bb, 0, jnp.maximum(i - 1, 0), 0)),
            seg_spec,
            pl.BlockSpec((d, d), lambda bb, i: (0, 0)),
            mod_spec,
            pl.BlockSpec((1, d), lambda bb, i: (0, 0)),
            mod_spec, mod_spec,
        ],
        out_specs=[seg_spec, seg_spec],
        out_shape=[jax.ShapeDtypeStruct(s.shape, F32), jax.ShapeDtypeStruct(s.shape, BF16)],
        compiler_params=_cparams(("arbitrary", "arbitrary")),
        name="attn_out_proj",
    )(o_ctx, o_lat, s, w, g1, ng, sh, sc)


def _wout_kernel(h0_ref, h1_ref, gate_ref, s_ref, w_ref, g1_ref, ng_ref, sh_ref, sc_ref,
                 rw_ref, tri_ref, snew_ref, f_ref, meta_ref, cnt_ref):
    @pl.when((pl.program_id(0) == 0) & (pl.program_id(1) == 0))
    def _():
        cnt_ref[...] = jnp.zeros(cnt_ref.shape, F32)

    a = ((h0_ref[0, 0] + h1_ref[0, 0]) * gate_ref[0]).astype(BF16)
    f = _tail(a, s_ref, w_ref, g1_ref, ng_ref, sh_ref, sc_ref, snew_ref, f_ref)
    meta_ref[0] = _route_top2(f, rw_ref, tri_ref, cnt_ref)


def _wout(hs, gate, s, w, g1, ng, sh, sc, rw, tri):
    b, ntok, d = s.shape
    nseg = ntok // SEG
    mod_spec = pl.BlockSpec((1, 1, d), lambda bb, i: (_mod_row(bb, i), 0, 0))
    seg_spec = pl.BlockSpec((1, SEG, d), lambda bb, i: (bb, i, 0))
    return pl.pallas_call(
        _wout_kernel,
        grid=(b, nseg),
        in_specs=[
            pl.BlockSpec((1, 1, SEG, d), lambda bb, i: (0, bb, i, 0)),
            pl.BlockSpec((1, 1, SEG, d), lambda bb, i: (1, bb, i, 0)),
            seg_spec, seg_spec,
            pl.BlockSpec((d, d), lambda bb, i: (0, 0)),
            mod_spec,
            pl.BlockSpec((1, d), lambda bb, i: (0, 0)),
            mod_spec, mod_spec,
            pl.BlockSpec((d, LANES), lambda bb, i: (0, 0)),
            pl.BlockSpec((SEG, SEG), lambda bb, i: (0, 0)),
        ],
        out_specs=[
            seg_spec,
            pl.BlockSpec((1, SEG, d // 2), lambda bb, i: (bb, i, 0)),
            pl.BlockSpec((1, SEG, LANES), lambda bb, i: (bb, i, 0)),
            pl.BlockSpec((8, LANES), lambda bb, i: (0, 0)),
        ],
        out_shape=[
            jax.ShapeDtypeStruct(s.shape, F32),
            jax.ShapeDtypeStruct((b, ntok, d // 2), jnp.uint32),
            jax.ShapeDtypeStruct((b, ntok, LANES), F32),
            jax.ShapeDtypeStruct((8, LANES), F32),
        ],
        compiler_params=_cparams(("arbitrary", "arbitrary")),
        name="lru_out_proj_router",
    )(hs, hs, gate, s, w, g1, ng, sh, sc, rw, tri)


def _swiglu_acc(x, wg_ref, wu_ref, wd_ref, acc_ref, j):
    @pl.when(j == 0)
    def _():
        acc_ref[...] = jnp.zeros(acc_ref.shape, F32)

    hg = jnp.dot(x, wg_ref[0], preferred_element_type=F32)
    hu = jnp.dot(x, wu_ref[0], preferred_element_type=F32)
    a = ((hg * jax.nn.sigmoid(hg)) * hu).astype(BF16)
    acc_ref[...] += jnp.dot(a, wd_ref[0], preferred_element_type=F32)


def _ffn_dense_kernel(f_ref, wg_ref, wu_ref, wd_ref, s_ref, gl_ref, gc_ref, o_ref, acc_ref):
    j = pl.program_id(2)
    _swiglu_acc(f_ref[0], wg_ref, wu_ref, wd_ref, acc_ref, j)

    @pl.when(j == pl.num_programs(2) - 1)
    def _():
        rows = lax.broadcasted_iota(jnp.int32, acc_ref.shape, 0)
        is_ctx = (rows < SEG) & (pl.program_id(1) == 0)
        gate = jnp.where(is_ctx, gc_ref[0], gl_ref[0])
        o_ref[0] = s_ref[0] + gate * acc_ref[...]


def _ffn_dense(f, wgu, wd, s, g2):
    b, ntok, d = s.shape
    ff = wd.shape[1]
    nf = ff // FFN_TF
    row_spec = pl.BlockSpec((1, FFN_TM, d), lambda bb, i, j: (bb, i, 0))
    return pl.pallas_call(
        _ffn_dense_kernel,
        grid=(b, ntok // FFN_TM, nf),
        in_specs=[
            row_spec,
            pl.BlockSpec((1, d, FFN_TF), lambda bb, i, j: (0, 0, j)),
            pl.BlockSpec((1, d, FFN_TF), lambda bb, i, j: (0, 0, j + nf)),
            pl.BlockSpec((1, FFN_TF, d), lambda bb, i, j: (0, j, 0)),
            row_spec,
            pl.BlockSpec((1, 1, d), lambda bb, i, j: (bb, 0, 0)),
            pl.BlockSpec((1, 1, d), lambda bb, i, j: (4, 0, 0)),
        ],
        out_specs=row_spec,
        out_shape=jax.ShapeDtypeStruct(s.shape, F32),
        scratch_shapes=[pltpu.VMEM((FFN_TM, d), F32)],
        compiler_params=_cparams(("arbitrary", "arbitrary", "arbitrary")),
        name="ffn_dense",
    )(f, wgu, wgu, wd, s, g2, g2)


def _ffn_moe_kernel(te_ref, nt_ref, x_ref, wg_ref, wu_ref, wd_ref, o_ref, acc_ref):
    i = pl.program_id(0)
    j = pl.program_id(1)

    @pl.when(i < nt_ref[0])
    def _():
        _swiglu_acc(_unpack_pairs(x_ref[...]).astype(BF16), wg_ref, wu_ref, wd_ref, acc_ref, j)

    @pl.when(j == pl.num_programs(1) - 1)
    def _():
        o_ref[...] = _pack_pairs(acc_ref[...])


def _ffn_moe(tile_expert, n_tiles, xs, wgu, wd):
    p = xs.shape[0]
    ff, d = wd.shape[1:]
    nf = ff // MOE_TF
    row_spec = pl.BlockSpec((MOE_TM, d // 2), lambda i, j, te, nt: (i, 0))
    grid_spec = pltpu.PrefetchScalarGridSpec(
        num_scalar_prefetch=2,
        grid=(p // MOE_TM, nf),
        in_specs=[
            row_spec,
            pl.BlockSpec((1, d, MOE_TF), lambda i, j, te, nt: (te[i], 0, j)),
            pl.BlockSpec((1, d, MOE_TF), lambda i, j, te, nt: (te[i], 0, j + nf)),
            pl.BlockSpec((1, MOE_TF, d), lambda i, j, te, nt: (te[i], j, 0)),
        ],
        out_specs=row_spec,
        scratch_shapes=[pltpu.VMEM((MOE_TM, d), F32)],
    )
    return pl.pallas_call(
        _ffn_moe_kernel,
        grid_spec=grid_spec,
        out_shape=jax.ShapeDtypeStruct((p, d // 2), jnp.uint32),
        compiler_params=_cparams(("arbitrary", "arbitrary")),
        name="ffn_experts",
    )(tile_expert, n_tiles, xs, wgu, wgu, wd)


def _row_copy(src_ref, src_row, dst_ref, dst_row, sem):
    return pltpu.make_async_copy(src_ref.at[pl.ds(src_row, 1), :], dst_ref.at[pl.ds(dst_row, 1), :], sem)


def _dispatch_kernel(pos_ref, f_ref, xs_in_ref, xs_ref, pos_smem, sem_pos, sem_rows):
    del xs_in_ref
    cp = pltpu.make_async_copy(pos_ref.at[0, 0], pos_smem, sem_pos)
    cp.start()
    cp.wait()

    def issue(r, c):
        _row_copy(f_ref, r, xs_ref, pos_smem[2 * r], sem_rows).start(priority=0)
        _row_copy(f_ref, r, xs_ref, pos_smem[2 * r + 1], sem_rows).start(priority=1)
        return c

    lax.fori_loop(0, SEG, issue, 0)

    def drain(r, c):
        _row_copy(f_ref, 0, xs_ref, 0, sem_rows).wait()
        _row_copy(f_ref, 0, xs_ref, 0, sem_rows).wait()
        return c

    lax.fori_loop(0, SEG, drain, 0)


def _dispatch(pos, f_rows, xs_init):
    t, d = f_rows.shape
    return pl.pallas_call(
        _dispatch_kernel,
        grid=(t // SEG,),
        in_specs=[
            pl.BlockSpec((1, 1, 2 * SEG), lambda i: (i, 0, 0)),
            pl.BlockSpec((SEG, d), lambda i: (i, 0)),
            pl.BlockSpec(memory_space=pl.ANY),
        ],
        out_specs=pl.BlockSpec(memory_space=pl.ANY),
        out_shape=jax.ShapeDtypeStruct(xs_init.shape, xs_init.dtype),
        scratch_shapes=[
            pltpu.SMEM((2 * SEG,), jnp.int32),
            pltpu.SemaphoreType.DMA,
            pltpu.SemaphoreType.DMA,
        ],
        input_output_aliases={2: 0},
        compiler_params=_cparams(("arbitrary",)),
        name="moe_dispatch",
    )(pos, f_rows, xs_init)


def _combine_kernel(pos_ref, meta_ref, s_ref, g2_ref, fg_ref, ys_ref, o_ref,
                    pos_smem, buf_ref, sem_pos, sem_rows, *, final):
    cp = pltpu.make_async_copy(pos_ref.at[0, 0], pos_smem, sem_pos)
    cp.start()
    cp.wait()

    def issue(r, c):
        _row_copy(ys_ref, pos_smem[2 * r], buf_ref.at[0], r, sem_rows).start(priority=0)
        _row_copy(ys_ref, pos_smem[2 * r + 1], buf_ref.at[1], r, sem_rows).start(priority=1)
        return c

    lax.fori_loop(0, SEG, issue, 0)

    def drain(r, c):
        _row_copy(ys_ref, 0, buf_ref.at[0], 0, sem_rows).wait()
        _row_copy(ys_ref, 0, buf_ref.at[1], 0, sem_rows).wait()
        return c

    lax.fori_loop(0, SEG, drain, 0)

    meta = meta_ref[0]
    y = meta[:, 2:3] * _unpack_pairs(buf_ref[0]) + meta[:, 3:4] * _unpack_pairs(buf_ref[1])
    s_new = s_ref[0] + g2_ref[0] * y
    if final:
        s_new = _rms(s_new) * fg_ref[...]
    o_ref[0] = s_new


def _combine(pos, meta, s, g2, final_g, ys, *, final):
    b, ntok, d = s.shape
    nseg = ntok // SEG
    off = 1 if final else 0
    nout = nseg - off
    kern = functools.partial(_combine_kernel, final=final)
    return pl.pallas_call(
        kern,
        grid=(b, nout),
        in_specs=[
            pl.BlockSpec((1, 1, 2 * SEG), lambda bb, i: (bb * nseg + i + off, 0, 0)),
            pl.BlockSpec((1, SEG, LANES), lambda bb, i: (bb, i + off, 0)),
            pl.BlockSpec((1, SEG, d), lambda bb, i: (bb, i + off, 0)),
            pl.BlockSpec((1, 1, d), lambda bb, i: (_mod_row(bb, i + off), 0, 0)),
            pl.BlockSpec((1, d), lambda bb, i: (0, 0)),
            pl.BlockSpec(memory_space=pl.ANY),
        ],
        out_specs=pl.BlockSpec((1, SEG, d), lambda bb, i: (bb, i, 0)),
        out_shape=jax.ShapeDtypeStruct((b, nout * SEG, d), F32),
        scratch_shapes=[
            pltpu.SMEM((2 * SEG,), jnp.int32),
            pltpu.VMEM((2, SEG, d // 2), jnp.uint32),
            pltpu.SemaphoreType.DMA,
            pltpu.SemaphoreType.DMA,
        ],
        compiler_params=_cparams(("arbitrary", "arbitrary")),
        name="moe_combine",
    )(pos, meta, s, g2, final_g, ys)


def _win_kernel(s_ref, g_ref, sh_ref, sc_ref, w_ref, gate_ref, u_ref):
    d = s_ref.shape[-1]
    h = _modulate(s_ref[0], g_ref[...], sh_ref[0], sc_ref[0]).astype(BF16)
    gate_ref[0] = jax.nn.gelu(jnp.dot(h, w_ref[:, :d], preferred_element_type=F32))
    u_ref[0] = jnp.dot(h, w_ref[:, d:], preferred_element_type=F32)


def _win(s, g, sh, sc, w):
    b, ntok, d = s.shape
    mod_spec = pl.BlockSpec((1, 1, d), lambda bb, i: (_mod_row(bb, i), 0, 0))
    seg_spec = pl.BlockSpec((1, SEG, d), lambda bb, i: (bb, i, 0))
    return pl.pallas_call(
        _win_kernel,
        grid=(b, ntok // SEG),
        in_specs=[
            seg_spec,
            pl.BlockSpec((1, d), lambda bb, i: (0, 0)),
            mod_spec, mod_spec,
            pl.BlockSpec((d, 2 * d), lambda bb, i: (0, 0)),
        ],
        out_specs=[seg_spec, seg_spec],
        out_shape=[jax.ShapeDtypeStruct(s.shape, F32), jax.ShapeDtypeStruct(s.shape, F32)],
        compiler_params=_cparams(("arbitrary", "arbitrary")),
        name="lru_in_proj",
    )(s, g, sh, sc, w)


def _lru_block(direction, step, nseg):
    return jnp.where((direction == 0) | (step == 0), step, nseg - step)


def _lru_kernel(u_ref, up_ref, un_ref, cw_ref, cb_ref, gw_ref, gb_ref, ap_ref, h_ref,
                a_ref, b_ref, st_ref, *, nseg):
    direction = pl.program_id(0)
    step = pl.program_id(2)
    blk = _lru_block(direction, step, nseg)
    d = u_ref.shape[-1]

    u = u_ref[0]
    has_prev = blk > 1
    has_next = (blk > 0) & (blk < nseg - 1)
    zero_row = jnp.zeros((1, d), F32)
    prev2 = jnp.where(has_prev, up_ref[0, 6:7, :], zero_row)
    prev1 = jnp.where(has_prev, up_ref[0, 7:8, :], zero_row)
    next1 = jnp.where(has_next, un_ref[0, 0:1, :], zero_row)
    rows = lax.broadcasted_iota(jnp.int32, u.shape, 0)
    um1 = jnp.where(rows == 0, prev1, pltpu.roll(u, 1, 0))
    um2 = jnp.where(rows == 0, prev2, jnp.where(rows == 1, prev1, pltpu.roll(u, 2, 0)))
    up1 = jnp.where(rows == SEG - 1, next1, pltpu.roll(u, SEG - 1, 0))
    cw = cw_ref[...]
    uc = cw[0:1] * um2 + cw[1:2] * um1 + cw[2:3] * u + cw[3:4] * up1 + cb_ref[...]

    ub = uc.astype(BF16)
    blkw = d // N_LRU_BLOCKS
    gr, gi = [], []
    for kb in range(N_LRU_BLOCKS):
        g = jnp.dot(ub[:, kb * blkw:(kb + 1) * blkw], gw_ref[0, kb], preferred_element_type=F32)
        gr.append(g[:, :blkw])
        gi.append(g[:, blkw:])
    gb = gb_ref[0]
    r_t = jax.nn.sigmoid(jnp.concatenate(gr, axis=1) + gb[0:1])
    i_t = jax.nn.sigmoid(jnp.concatenate(gi, axis=1) + gb[1:2])
    ap = ap_ref[0]
    log_sig = jnp.minimum(ap, 0.0) - jnp.log1p(jnp.exp(-jnp.abs(ap)))
    log_a = RG_C * r_t * log_sig
    a_t = jnp.exp(log_a)
    a_ref[...] = a_t
    b_ref[...] = jnp.sqrt(1.0 - a_t * a_t) * (i_t * uc)

    @pl.when(step == 0)
    def _():
        st_ref[...] = jnp.zeros(st_ref.shape, F32)

    def run(reverse):
        def body(t, hprev):
            tt = SEG - 1 - t if reverse else t
            hnew = a_ref[pl.ds(tt, 1), :] * hprev + b_ref[pl.ds(tt, 1), :]
            h_ref[0, 0, pl.ds(tt, 1), :] = hnew
            return hnew
        st_ref[...] = lax.fori_loop(0, SEG, body, st_ref[...], unroll=8)

    @pl.when(direction == 0)
    def _():
        run(False)

    @pl.when(direction == 1)
    def _():
        run(True)


def _lru(u, conv_w, conv_b, gate_w, gate_b, a_param):
    b, ntok, d = u.shape
    nseg = ntok // SEG
    sub = SEG // 8
    nsub = ntok // 8
    blkw = d // N_LRU_BLOCKS

    def blk(dd, i):
        return _lru_block(dd, i, nseg)

    kern = functools.partial(_lru_kernel, nseg=nseg)
    return pl.pallas_call(
        kern,
        grid=(2, b, nseg),
        in_specs=[
            pl.BlockSpec((1, SEG, d), lambda dd, bb, i: (bb, blk(dd, i), 0)),
            pl.BlockSpec((1, 8, d), lambda dd, bb, i: (bb, jnp.maximum(blk(dd, i) * sub - 1, 0), 0)),
            pl.BlockSpec((1, 8, d), lambda dd, bb, i: (bb, jnp.minimum((blk(dd, i) + 1) * sub, nsub - 1), 0)),
            pl.BlockSpec((4, d), lambda dd, bb, i: (0, 0)),
            pl.BlockSpec((1, d), lambda dd, bb, i: (0, 0)),
            pl.BlockSpec((1, N_LRU_BLOCKS, blkw, 2 * blkw), lambda dd, bb, i: (dd, 0, 0, 0)),
            pl.BlockSpec((1, 2, d), lambda dd, bb, i: (dd, 0, 0)),
            pl.BlockSpec((1, 1, d), lambda dd, bb, i: (dd, 0, 0)),
        ],
        out_specs=pl.BlockSpec((1, 1, SEG, d), lambda dd, bb, i: (dd, bb, blk(dd, i), 0)),
        out_shape=jax.ShapeDtypeStruct((2, b, ntok, d), F32),
        scratch_shapes=[
            pltpu.VMEM((SEG, d), F32),
            pltpu.VMEM((SEG, d), F32),
            pltpu.VMEM((1, d), F32),
        ],
        compiler_params=_cparams(("arbitrary", "arbitrary", "arbitrary")),
        name="rglru_scan",
    )(u, u, u, conv_w, conv_b, gate_w, gate_b, a_param)


def _rope_tables(n_ctx, n_lat):
    rows = n_lat // GRID_W
    row = jnp.repeat(jnp.arange(rows, dtype=F32), GRID_W)
    col = jnp.tile(jnp.arange(GRID_W, dtype=F32), rows)
    inv_freq = 1.0 / (ROPE_BASE ** (jnp.arange(0, ROPE_AXIS_DIM, 2, dtype=F32) / ROPE_AXIS_DIM))
    ang = jnp.stack([row[:, None] * inv_freq, col[:, None] * inv_freq], axis=1)
    ang = jnp.broadcast_to(ang[:, :, None, :], (n_lat, 2, 2, ROPE_AXIS_DIM // 2)).reshape(n_lat, HEAD_DIM)
    ang = jnp.concatenate([jnp.zeros((n_ctx, HEAD_DIM), F32), ang], axis=0)
    cos = jnp.cos(ang)
    sin = jnp.sin(ang)
    lane = jnp.arange(HEAD_DIM)
    sin = jnp.where((lane % ROPE_AXIS_DIM) < (ROPE_AXIS_DIM // 2), -sin, sin)
    return jnp.tile(cos, (1, 2)), jnp.tile(sin, (1, 2))


def kernel(x, c, ctx, c_ctx, mod_w, mod_b, norm_mix_g, norm_ffn_g, attn_w_qkv, attn_w_o, attn_lambda, attn_subln_g, lru_w_in, lru_conv_w, lru_conv_b, lru_gate_w, lru_gate_b, lru_a_param, lru_w_out, ffn_w_gate_up, ffn_w_down, moe_router_w, moe_w_gate_up, moe_w_down, final_norm_g):
    bsz, n_lat, d = x.shape
    n_ctx = ctx.shape[1]
    depth = mod_w.shape[0]
    assert n_ctx == SEG and n_lat % ATT_TQ == 0 and bsz <= 4
    ntok = n_ctx + n_lat
    assert ntok % ATT_TK == 0 and ntok % FFN_TM == 0
    nseg = ntok // SEG
    t_all = bsz * ntok

    s = jnp.concatenate([ctx, x], axis=1)
    c_pad = jnp.zeros((8, d), F32).at[:bsz].set(c).at[4].set(c_ctx)
    mod = _modulation(c_pad, mod_w, mod_b)
    cos, sin = _rope_tables(n_ctx, n_lat)
    tri = (jnp.arange(SEG)[:, None] > jnp.arange(SEG)[None, :]).astype(BF16)
    blkw = d // N_LRU_BLOCKS
    out = None

    for i in range(depth):
        j = i // 2
        last = i == depth - 1
        sh1, sc1, g1, sh2, sc2, g2 = [mod[i, :, k * d:(k + 1) * d].reshape(8, 1, d) for k in range(6)]
        ng_mix = norm_mix_g[i].reshape(1, d)
        ng_ffn = norm_ffn_g[i].reshape(1, d)
        if i % 2 == 0:
            lambda_init = 0.8 - 0.6 * math.exp(-0.3 * i)
            qt, k, vt = _qkv(s, ng_mix, sh1, sc1, attn_w_qkv[j].astype(BF16), cos, sin)
            subln = attn_subln_g[j].reshape(1, HEAD_W)
            o_ctx = _attention(attn_lambda[j], subln, qt, k, vt, q_off=0, nq=n_ctx,
                               n_chunks=1, tk=SEG, lambda_init=lambda_init)
            o_lat = _attention(attn_lambda[j], subln, qt, k, vt, q_off=n_ctx, nq=n_lat,
                               n_chunks=ntok // ATT_TK, tk=ATT_TK, lambda_init=lambda_init)
            s, f = _wo(o_ctx, o_lat, s, attn_w_o[j].astype(BF16), g1, ng_ffn, sh2, sc2)
            s = _ffn_dense(f, ffn_w_gate_up[j].astype(BF16)[None], ffn_w_down[j].astype(BF16)[None], s, g2)
        else:
            gate, u = _win(s, ng_mix, sh1, sc1, lru_w_in[j].astype(BF16))
            gw = lru_gate_w[j].transpose(0, 2, 3, 1, 4).reshape(2, N_LRU_BLOCKS, blkw, 2 * blkw).astype(BF16)
            hs = _lru(u, lru_conv_w[j], lru_conv_b[j].reshape(1, d), gw, lru_gate_b[j],
                      lru_a_param[j].reshape(2, 1, d))
            rw = jnp.zeros((d, LANES), BF16).at[:, :N_EXPERTS].set(moe_router_w[j].astype(BF16))
            s, f, meta, cnt = _wout(hs, gate, s, lru_w_out[j].astype(BF16), g1, ng_ffn, sh2, sc2, rw, tri)

            meta2 = meta.reshape(t_all, LANES)
            idx = meta2[:, 0:2].astype(jnp.int32)
            rank = meta2[:, 4:6].astype(jnp.int32)
            counts = cnt[0, :N_EXPERTS].astype(jnp.int32)
            padded = ((counts + MOE_TM - 1) // MOE_TM) * MOE_TM
            ends = jnp.cumsum(padded)
            starts = ends - padded
            pos = (starts[idx] + rank).reshape(t_all // SEG, 1, 2 * SEG)
            p_rows = 2 * t_all + N_EXPERTS * MOE_TM
            tile_start = jnp.arange(p_rows // MOE_TM, dtype=jnp.int32) * MOE_TM
            tile_expert = jnp.minimum(jnp.sum((tile_start[:, None] >= ends[None, :]).astype(jnp.int32), axis=1),
                                      N_EXPERTS - 1)
            n_tiles = (ends[-1:] // MOE_TM).astype(jnp.int32)

            xs = _dispatch(pos, f.reshape(t_all, d // 2), jnp.zeros((p_rows, d // 2), jnp.uint32))
            ys = _ffn_moe(tile_expert, n_tiles, xs, moe_w_gate_up[j].astype(BF16), moe_w_down[j].astype(BF16))
            res = _combine(pos, meta, s, g2, final_norm_g.reshape(1, d), ys, final=last)
            if last:
                out = res
            else:
                s = res
    return out
```

```python
import functools
import math

import jax
import jax.numpy as jnp
from jax import lax
from jax.experimental import pallas as pl
from jax.experimental.pallas import tpu as pltpu

F32 = jnp.float32
BF16 = jnp.bfloat16

N_HEADS = 8
HEAD_DIM = 64
HEAD_W = 2 * HEAD_DIM
GRID_W = 64
ROPE_BASE = 10000.0
ROPE_AXIS_DIM = HEAD_DIM // 2
N_LRU_BLOCKS = 8
RG_C = 8.0
N_EXPERTS = 8
EPS = 1e-6
QK_SCALE = HEAD_DIM ** -0.5 * math.log2(math.e)

LANES = 128
SEG = 256
ATT_TQ = 256
ATT_TK = 768
FFN_TM = 768
FFN_TF = 1408
MOE_TM = 512
MOE_TF = 1792
MOD_TN = 1536
VMEM_LIMIT = 56 * 1024 * 1024


def _cparams(sem):
    return pltpu.CompilerParams(dimension_semantics=sem, vmem_limit_bytes=VMEM_LIMIT)


def _rms(x):
    return x * lax.rsqrt(jnp.mean(x * x, axis=-1, keepdims=True) + EPS)


def _modulate(x, g, shift, scale):
    return (_rms(x) * g) * (1.0 + scale) + shift


def _pack_pairs(x):
    bits = pltpu.bitcast(x.astype(BF16).astype(F32), jnp.uint32)
    c = x.shape[1] // 2
    return (bits[:, :c] >> 16) | bits[:, c:]


def _unpack_pairs(u):
    lo = pltpu.bitcast(u << 16, F32)
    hi = pltpu.bitcast(u & jnp.uint32(0xFFFF0000), F32)
    return jnp.concatenate([lo, hi], axis=1)


def _mod_row(b, i):
    return jnp.where(i == 0, 4, b)


def _mod_kernel(c_ref, w_ref, b_ref, o_ref):
    c = c_ref[...]
    a = (c * jax.nn.sigmoid(c)).astype(BF16)
    o_ref[0] = jnp.dot(a, w_ref[0].astype(BF16), preferred_element_type=F32) + b_ref[0]


def _modulation(c_pad, mod_w, mod_b):
    depth, d, n = mod_w.shape
    return pl.pallas_call(
        _mod_kernel,
        grid=(depth, n // MOD_TN),
        in_specs=[
            pl.BlockSpec((8, d), lambda l, j: (0, 0)),
            pl.BlockSpec((1, d, MOD_TN), lambda l, j: (l, 0, j)),
            pl.BlockSpec((1, 1, MOD_TN), lambda l, j: (l, 0, j)),
        ],
        out_specs=pl.BlockSpec((1, 8, MOD_TN), lambda l, j: (l, 0, j)),
        out_shape=jax.ShapeDtypeStruct((depth, 8, n), F32),
        compiler_params=_cparams(("arbitrary", "arbitrary")),
        name="modulation",
    )(c_pad, mod_w, mod_b.reshape(depth, 1, n))


def _qkv_kernel(s_ref, g_ref, sh_ref, sc_ref, w_ref, cos_ref, sin_ref, qt_ref, k_ref, vt_ref):
    d = s_ref.shape[-1]
    h = _modulate(s_ref[0], g_ref[...], sh_ref[0], sc_ref[0]).astype(BF16)
    cos = cos_ref[...]
    sin = sin_ref[...]
    lane = lax.broadcasted_iota(jnp.int32, cos.shape, 1)
    first = (lane % ROPE_AXIS_DIM) < (ROPE_AXIS_DIM // 2)

    def rope(x):
        rot = jnp.where(first, pltpu.roll(x, LANES - ROPE_AXIS_DIM // 2, 1),
                        pltpu.roll(x, ROPE_AXIS_DIM // 2, 1))
        return x * cos + rot * sin

    pair = 2 * HEAD_W
    for c in range(N_HEADS // 2):
        rq = jnp.dot(h, w_ref[:, c * pair:(c + 1) * pair], preferred_element_type=F32)
        rk = jnp.dot(h, w_ref[:, d + c * pair:d + (c + 1) * pair], preferred_element_type=F32)
        rv = jnp.dot(h, w_ref[:, 2 * d + c * pair:2 * d + (c + 1) * pair], preferred_element_type=F32)
        for u in range(2):
            hh = 2 * c + u
            sl = slice(u * HEAD_W, (u + 1) * HEAD_W)
            q = rope(rq[:, sl]) * QK_SCALE
            qt_ref[0, hh, 0] = q.T.astype(BF16)
            k_ref[0, hh] = rope(rk[:, sl]).astype(BF16)
            vt_ref[0, hh, 0] = rv[:, sl].T.astype(BF16)


def _qkv(s, g, sh, sc, w, cos, sin):
    b, ntok, d = s.shape
    nseg = ntok // SEG
    per = ATT_TK // SEG
    mod_spec = pl.BlockSpec((1, 1, d), lambda bb, i: (_mod_row(bb, i), 0, 0))
    return pl.pallas_call(
        _qkv_kernel,
        grid=(b, nseg),
        in_specs=[
            pl.BlockSpec((1, SEG, d), lambda bb, i: (bb, i, 0)),
            pl.BlockSpec((1, d), lambda bb, i: (0, 0)),
            mod_spec, mod_spec,
            pl.BlockSpec((d, 3 * d), lambda bb, i: (0, 0)),
            pl.BlockSpec((SEG, HEAD_W), lambda bb, i: (i, 0)),
            pl.BlockSpec((SEG, HEAD_W), lambda bb, i: (i, 0)),
        ],
        out_specs=[
            pl.BlockSpec((1, N_HEADS, 1, HEAD_W, SEG), lambda bb, i: (bb, 0, i, 0, 0)),
            pl.BlockSpec((1, N_HEADS, SEG, HEAD_W), lambda bb, i: (bb, 0, i, 0)),
            pl.BlockSpec((1, N_HEADS, 1, HEAD_W, SEG), lambda bb, i: (bb, 0, i // per, 0, i % per)),
        ],
        out_shape=[
            jax.ShapeDtypeStruct((b, N_HEADS, nseg, HEAD_W, SEG), BF16),
            jax.ShapeDtypeStruct((b, N_HEADS, ntok, HEAD_W), BF16),
            jax.ShapeDtypeStruct((b, N_HEADS, ntok // ATT_TK, HEAD_W, ATT_TK), BF16),
        ],
        compiler_params=_cparams(("arbitrary", "arbitrary")),
        name="qkv_rope",
    )(s, g, sh, sc, w, cos, sin)


def _attn_kernel(lam_ref, g_ref, qt_ref, k_ref, vt_ref, o_ref, sa_ref, sb_ref, l_ref, acc_ref,
                 *, n_qt, q_blk, n_chunks, tk, lambda_init):
    n_steps = n_qt * n_chunks
    row = lax.broadcasted_iota(jnp.int32, (HEAD_W, ATT_TQ), 0)
    zero = jnp.zeros((HEAD_W, ATT_TQ), BF16)
    neg_inf = jnp.full((1, ATT_TQ), -jnp.inf, F32)

    lv = lam_ref[...]
    lam = (jnp.exp(jnp.sum(lv[0:1] * lv[1:2], axis=1, keepdims=True))
           - jnp.exp(jnp.sum(lv[2:3] * lv[3:4], axis=1, keepdims=True)) + lambda_init)

    l_ref[...] = jnp.zeros(l_ref.shape, F32)
    acc_ref[...] = jnp.zeros(acc_ref.shape, F32)

    def stage_a(t, s_ref, m_run):
        t = jnp.minimum(t, n_steps - 1)
        qi = t // n_chunks
        c = t - qi * n_chunks
        qt = qt_ref[0, 0, qi + q_blk]
        qparts = (jnp.where(row < HEAD_DIM, qt, zero), jnp.where(row >= HEAD_DIM, qt, zero))
        kc = k_ref[0, 0, pl.ds(pl.multiple_of(c * tk, tk), tk), :]
        m_new, alpha = [], []
        for p in range(2):
            s = jnp.dot(kc, qparts[p], preferred_element_type=F32)
            s_ref[p] = s
            m_in = jnp.where(c == 0, neg_inf, m_run[p])
            mx = jnp.maximum(m_in, jnp.max(s, axis=0, keepdims=True))
            alpha.append(jnp.exp2(m_in - mx))
            m_new.append(mx)
        return tuple(m_new), tuple(alpha)

    def stage_b(t, s_ref, m_run, alpha):
        qi = t // n_chunks
        c = t - qi * n_chunks
        slot = qi % 2
        vc = vt_ref[0, 0, c]
        for p in range(2):
            pr = jnp.exp2(s_ref[p] - m_run[p])
            l_ref[slot, p] = alpha[p] * l_ref[slot, p] + jnp.sum(pr, axis=0, keepdims=True)
            acc_ref[slot, p] = (alpha[p] * acc_ref[slot, p]
                                + jnp.dot(vc, pr.astype(BF16), preferred_element_type=F32))

    def finish(qi):
        slot = qi % 2
        ot = acc_ref[slot, 0] / l_ref[slot, 0] - lam * (acc_ref[slot, 1] / l_ref[slot, 1])
        ot = ot * lax.rsqrt(jnp.mean(ot * ot, axis=0, keepdims=True) + EPS)
        o = ((ot.T * g_ref[...]) * (1.0 - lambda_init)).astype(o_ref.dtype)
        o_ref[0, 0, pl.ds(pl.multiple_of(qi * ATT_TQ, ATT_TQ), ATT_TQ), :] = o

    m_run, alpha = stage_a(0, sa_ref, (neg_inf, neg_inf))
    if n_steps == 1:
        stage_b(jnp.int32(0), sa_ref, m_run, alpha)
        finish(jnp.int32(0))
        return

    unroll = 4 if n_steps % 4 == 0 else 2
    assert n_steps % unroll == 0 and unroll <= n_chunks
    bufs = (sa_ref, sb_ref)

    def group(i, carry):
        m_run, alpha = carry
        t0 = i * unroll
        for u in range(unroll):
            m_nxt, a_nxt = stage_a(t0 + u + 1, bufs[(u + 1) % 2], m_run)
            stage_b(t0 + u, bufs[u % 2], m_run, alpha)
            m_run, alpha = m_nxt, a_nxt
        done = (t0 + unroll) // n_chunks

        @pl.when(done > t0 // n_chunks)
        def _():
            finish(done - 1)

        return m_run, alpha

    lax.fori_loop(0, n_steps // unroll, group, (m_run, alpha))


def _attention(lam_vec, subln_g, qt, k, vt, *, q_off, nq, n_chunks, tk, lambda_init):
    b, nh, nqt_all, _, _ = qt.shape
    kern = functools.partial(_attn_kernel, n_qt=nq // ATT_TQ, q_blk=q_off // ATT_TQ,
                             n_chunks=n_chunks, tk=tk, lambda_init=lambda_init)
    vt_block = (1, 1, n_chunks, HEAD_W, tk)
    return pl.pallas_call(
        kern,
        grid=(b, nh),
        in_specs=[
            pl.BlockSpec((4, HEAD_DIM), lambda bb, h: (0, 0)),
            pl.BlockSpec((1, HEAD_W), lambda bb, h: (0, 0)),
            pl.BlockSpec((1, 1, nqt_all, HEAD_W, ATT_TQ), lambda bb, h: (bb, h, 0, 0, 0)),
            pl.BlockSpec((1, 1, n_chunks * tk, HEAD_W), lambda bb, h: (bb, h, 0, 0)),
            pl.BlockSpec(vt_block, lambda bb, h: (bb, h, 0, 0, 0)),
        ],
        out_specs=pl.BlockSpec((1, 1, nq, HEAD_W), lambda bb, h: (bb, h, 0, 0)),
        out_shape=jax.ShapeDtypeStruct((b, nh, nq, HEAD_W), BF16),
        scratch_shapes=[
            pltpu.VMEM((2, tk, ATT_TQ), F32),
            pltpu.VMEM((2, tk, ATT_TQ), F32),
            pltpu.VMEM((2, 2, 1, ATT_TQ), F32),
            pltpu.VMEM((2, 2, HEAD_W, ATT_TQ), F32),
        ],
        compiler_params=_cparams(("arbitrary", "arbitrary")),
        name="diff_attention",
    )(lam_vec, subln_g, qt, k, vt)


def _route_top2(f, rw_ref, tri_ref, cnt_ref):
    logits = jnp.dot(f.astype(BF16), rw_ref[...], preferred_element_type=F32)
    lane = lax.broadcasted_iota(jnp.int32, logits.shape, 1).astype(F32)
    neg = jnp.full(logits.shape, -jnp.inf, F32)
    lg = jnp.where(lane < N_EXPERTS, logits, neg)
    m1 = jnp.max(lg, axis=1, keepdims=True)
    i1 = jnp.min(jnp.where(lg == m1, lane, float(LANES)), axis=1, keepdims=True)
    lg2 = jnp.where(lane == i1, neg, lg)
    m2 = jnp.max(lg2, axis=1, keepdims=True)
    i2 = jnp.min(jnp.where(lg2 == m2, lane, float(LANES)), axis=1, keepdims=True)
    e = jnp.exp(m2 - m1)
    w1 = 1.0 / (1.0 + e)
    w2 = e / (1.0 + e)
    oh1 = (lane == i1).astype(F32)
    oh2 = (lane == i2).astype(F32)
    cnt = oh1 + oh2
    before = jnp.dot(tri_ref[...], cnt.astype(BF16), preferred_element_type=F32) + cnt_ref[0:1, :]
    r1 = jnp.sum(oh1 * before, axis=1, keepdims=True)
    r2 = jnp.sum(oh2 * before, axis=1, keepdims=True)
    cnt_ref[...] = cnt_ref[...] + jnp.sum(cnt, axis=0, keepdims=True)
    meta = jnp.zeros(logits.shape, F32)
    for j, val in enumerate((i1, i2, w1, w2, r1, r2)):
        meta = jnp.where(lane == float(j), val, meta)
    return meta


def _tail(a, s_ref, w_ref, g1_ref, ng_ref, sh_ref, sc_ref, snew_ref, f_ref):
    y = jnp.dot(a, w_ref[...], preferred_element_type=F32)
    s_new = s_ref[0] + g1_ref[0] * y
    snew_ref[0] = s_new
    f = _modulate(s_new, ng_ref[...], sh_ref[0], sc_ref[0])
    f_ref[0] = _pack_pairs(f) if f_ref.dtype == jnp.uint32 else f.astype(f_ref.dtype)
    return f


def _wo_kernel(oc_ref, ol_ref, s_ref, w_ref, g1_ref, ng_ref, sh_ref, sc_ref, snew_ref, f_ref):
    is_ctx = pl.program_id(1) == 0
    a = jnp.concatenate(
        [jnp.where(is_ctx, oc_ref[0, h], ol_ref[0, h]) for h in range(N_HEADS)], axis=1)
    _tail(a, s_ref, w_ref, g1_ref, ng_ref, sh_ref, sc_ref, snew_ref, f_ref)


def _wo(o_ctx, o_lat, s, w, g1, ng, sh, sc):
    b, ntok, d = s.shape
    nseg = ntok // SEG
    mod_spec = pl.BlockSpec((1, 1, d), lambda bb, i: (_mod_row(bb, i), 0, 0))
    seg_spec = pl.BlockSpec((1, SEG, d), lambda bb, i: (bb, i, 0))
    return pl.pallas_call(
        _wo_kernel,
        grid=(b, nseg),
        in_specs=[
            pl.BlockSpec((1, N_HEADS, SEG, HEAD_W), lambda bb, i: (bb, 0, 0, 0)),
            pl.BlockSpec((1, N_HEADS, SEG, HEAD_W), lambda bb, i: (bb, 0, jnp.maximum(i - 1, 0), 0)),
            seg_spec,
            pl.BlockSpec((d, d), lambda bb, i: (0, 0)),
            mod_spec,
            pl.BlockSpec((1, d), lambda bb, i: (0, 0)),
            mod_spec, mod_spec,
        ],
        out_specs=[seg_spec, seg_spec],
        out_shape=[jax.ShapeDtypeStruct(s.shape, F32), jax.ShapeDtypeStruct(s.shape, BF16)],
        compiler_params=_cparams(("arbitrary", "arbitrary")),
        name="attn_out_proj",
    )(o_ctx, o_lat, s, w, g1, ng, sh, sc)


def _wout_kernel(h0_ref, h1_ref, gate_ref, s_ref, w_ref, g1_ref, ng_ref, sh_ref, sc_ref,
                 rw_ref, tri_ref, snew_ref, f_ref, meta_ref, cnt_ref):
    @pl.when((pl.program_id(0) == 0) & (pl.program_id(1) == 0))
    def _():
        cnt_ref[...] = jnp.zeros(cnt_ref.shape, F32)

    a = ((h0_ref[0, 0] + h1_ref[0, 0]) * gate_ref[0]).astype(BF16)
    f = _tail(a, s_ref, w_ref, g1_ref, ng_ref, sh_ref, sc_ref, snew_ref, f_ref)
    meta_ref[0] = _route_top2(f, rw_ref, tri_ref, cnt_ref)


def _wout(hs, gate, s, w, g1, ng, sh, sc, rw, tri):
    b, ntok, d = s.shape
    nseg = ntok // SEG
    mod_spec = pl.BlockSpec((1, 1, d), lambda bb, i: (_mod_row(bb, i), 0, 0))
    seg_spec = pl.BlockSpec((1, SEG, d), lambda bb, i: (bb, i, 0))
    return pl.pallas_call(
        _wout_kernel,
        grid=(b, nseg),
        in_specs=[
            pl.BlockSpec((1, 1, SEG, d), lambda bb, i: (0, bb, i, 0)),
            pl.BlockSpec((1, 1, SEG, d), lambda bb, i: (1, bb, i, 0)),
            seg_spec, seg_spec,
            pl.BlockSpec((d, d), lambda bb, i: (0, 0)),
            mod_spec,
            pl.BlockSpec((1, d), lambda bb, i: (0, 0)),
            mod_spec, mod_spec,
            pl.BlockSpec((d, LANES), lambda bb, i: (0, 0)),
            pl.BlockSpec((SEG, SEG), lambda bb, i: (0, 0)),
        ],
        out_specs=[
            seg_spec,
            pl.BlockSpec((1, SEG, d // 2), lambda bb, i: (bb, i, 0)),
            pl.BlockSpec((1, SEG, LANES), lambda bb, i: (bb, i, 0)),
            pl.BlockSpec((8, LANES), lambda bb, i: (0, 0)),
        ],
        out_shape=[
            jax.ShapeDtypeStruct(s.shape, F32),
            jax.ShapeDtypeStruct((b, ntok, d // 2), jnp.uint32),
            jax.ShapeDtypeStruct((b, ntok, LANES), F32),
            jax.ShapeDtypeStruct((8, LANES), F32),
        ],
        compiler_params=_cparams(("arbitrary", "arbitrary")),
        name="lru_out_proj_router",
    )(hs, hs, gate, s, w, g1, ng, sh, sc, rw, tri)


def _swiglu_acc(x, wg_ref, wu_ref, wd_ref, acc_ref, j):
    @pl.when(j == 0)
    def _():
        acc_ref[...] = jnp.zeros(acc_ref.shape, F32)

    hg = jnp.dot(x, wg_ref[0], preferred_element_type=F32)
    hu = jnp.dot(x, wu_ref[0], preferred_element_type=F32)
    a = ((hg * jax.nn.sigmoid(hg)) * hu).astype(BF16)
    acc_ref[...] += jnp.dot(a, wd_ref[0], preferred_element_type=F32)


def _ffn_dense_kernel(f_ref, wg_ref, wu_ref, wd_ref, s_ref, gl_ref, gc_ref, o_ref, acc_ref):
    j = pl.program_id(2)
    _swiglu_acc(f_ref[0], wg_ref, wu_ref, wd_ref, acc_ref, j)

    @pl.when(j == pl.num_programs(2) - 1)
    def _():
        rows = lax.broadcasted_iota(jnp.int32, acc_ref.shape, 0)
        is_ctx = (rows < SEG) & (pl.program_id(1) == 0)
        gate = jnp.where(is_ctx, gc_ref[0], gl_ref[0])
        o_ref[0] = s_ref[0] + gate * acc_ref[...]


def _ffn_dense(f, wgu, wd, s, g2):
    b, ntok, d = s.shape
    ff = wd.shape[1]
    nf = ff // FFN_TF
    row_spec = pl.BlockSpec((1, FFN_TM, d), lambda bb, i, j: (bb, i, 0))
    return pl.pallas_call(
        _ffn_dense_kernel,
        grid=(b, ntok // FFN_TM, nf),
        in_specs=[
            row_spec,
            pl.BlockSpec((1, d, FFN_TF), lambda bb, i, j: (0, 0, j)),
            pl.BlockSpec((1, d, FFN_TF), lambda bb, i, j: (0, 0, j + nf)),
            pl.BlockSpec((1, FFN_TF, d), lambda bb, i, j: (0, j, 0)),
            row_spec,
            pl.BlockSpec((1, 1, d), lambda bb, i, j: (bb, 0, 0)),
            pl.BlockSpec((1, 1, d), lambda bb, i, j: (4, 0, 0)),
        ],
        out_specs=row_spec,
        out_shape=jax.ShapeDtypeStruct(s.shape, F32),
        scratch_shapes=[pltpu.VMEM((FFN_TM, d), F32)],
        compiler_params=_cparams(("arbitrary", "arbitrary", "arbitrary")),
        name="ffn_dense",
    )(f, wgu, wgu, wd, s, g2, g2)


def _ffn_moe_kernel(te_ref, nt_ref, x_ref, wg_ref, wu_ref, wd_ref, o_ref, acc_ref):
    i = pl.program_id(0)
    j = pl.program_id(1)

    @pl.when(i < nt_ref[0])
    def _():
        _swiglu_acc(_unpack_pairs(x_ref[...]).astype(BF16), wg_ref, wu_ref, wd_ref, acc_ref, j)

    @pl.when(j == pl.num_programs(1) - 1)
    def _():
        o_ref[...] = _pack_pairs(acc_ref[...])


def _ffn_moe(tile_expert, n_tiles, xs, wgu, wd):
    p = xs.shape[0]
    ff, d = wd.shape[1:]
    nf = ff // MOE_TF
    row_spec = pl.BlockSpec((MOE_TM, d // 2), lambda i, j, te, nt: (i, 0))
    grid_spec = pltpu.PrefetchScalarGridSpec(
        num_scalar_prefetch=2,
        grid=(p // MOE_TM, nf),
        in_specs=[
            row_spec,
            pl.BlockSpec((1, d, MOE_TF), lambda i, j, te, nt: (te[i], 0, j)),
            pl.BlockSpec((1, d, MOE_TF), lambda i, j, te, nt: (te[i], 0, j + nf)),
            pl.BlockSpec((1, MOE_TF, d), lambda i, j, te, nt: (te[i], j, 0)),
        ],
        out_specs=row_spec,
        scratch_shapes=[pltpu.VMEM((MOE_TM, d), F32)],
    )
    return pl.pallas_call(
        _ffn_moe_kernel,
        grid_spec=grid_spec,
        out_shape=jax.ShapeDtypeStruct((p, d // 2), jnp.uint32),
        compiler_params=_cparams(("arbitrary", "arbitrary")),
        name="ffn_experts",
    )(tile_expert, n_tiles, xs, wgu, wgu, wd)


def _row_copy(src_ref, src_row, dst_ref, dst_row, sem):
    return pltpu.make_async_copy(src_ref.at[pl.ds(src_row, 1), :], dst_ref.at[pl.ds(dst_row, 1), :], sem)


def _dispatch_kernel(pos_ref, f_ref, xs_in_ref, xs_ref, pos_smem, sem_pos, sem_rows):
    del xs_in_ref
    cp = pltpu.make_async_copy(pos_ref.at[0, 0], pos_smem, sem_pos)
    cp.start()
    cp.wait()

    def issue(r, c):
        _row_copy(f_ref, r, xs_ref, pos_smem[2 * r], sem_rows).start(priority=0)
        _row_copy(f_ref, r, xs_ref, pos_smem[2 * r + 1], sem_rows).start(priority=1)
        return c

    lax.fori_loop(0, SEG, issue, 0, unroll=8)

    def drain(r, c):
        _row_copy(f_ref, 0, xs_ref, 0, sem_rows).wait()
        _row_copy(f_ref, 0, xs_ref, 0, sem_rows).wait()
        return c

    lax.fori_loop(0, SEG, drain, 0, unroll=8)


def _dispatch(pos, f_rows, xs_init):
    t, d = f_rows.shape
    return pl.pallas_call(
        _dispatch_kernel,
        grid=(t // SEG,),
        in_specs=[
            pl.BlockSpec((1, 1, 2 * SEG), lambda i: (i, 0, 0)),
            pl.BlockSpec((SEG, d), lambda i: (i, 0)),
            pl.BlockSpec(memory_space=pl.ANY),
        ],
        out_specs=pl.BlockSpec(memory_space=pl.ANY),
        out_shape=jax.ShapeDtypeStruct(xs_init.shape, xs_init.dtype),
        scratch_shapes=[
            pltpu.SMEM((2 * SEG,), jnp.int32),
            pltpu.SemaphoreType.DMA,
            pltpu.SemaphoreType.DMA,
        ],
        input_output_aliases={2: 0},
        compiler_params=_cparams(("arbitrary",)),
        name="moe_dispatch",
    )(pos, f_rows, xs_init)


def _combine_kernel(pos_ref, meta_ref, s_ref, g2_ref, fg_ref, ys_ref, o_ref,
                    pos_smem, buf_ref, sem_pos, sem_rows, *, final):
    cp = pltpu.make_async_copy(pos_ref.at[0, 0], pos_smem, sem_pos)
    cp.start()
    cp.wait()

    def issue(r, c):
        _row_copy(ys_ref, pos_smem[2 * r], buf_ref.at[0], r, sem_rows).start(priority=0)
        _row_copy(ys_ref, pos_smem[2 * r + 1], buf_ref.at[1], r, sem_rows).start(priority=1)
        return c

    lax.fori_loop(0, SEG, issue, 0, unroll=8)

    def drain(r, c):
        _row_copy(ys_ref, 0, buf_ref.at[0], 0, sem_rows).wait()
        _row_copy(ys_ref, 0, buf_ref.at[1], 0, sem_rows).wait()
        return c

    lax.fori_loop(0, SEG, drain, 0, unroll=8)

    meta = meta_ref[0]
    y = meta[:, 2:3] * _unpack_pairs(buf_ref[0]) + meta[:, 3:4] * _unpack_pairs(buf_ref[1])
    s_new = s_ref[0] + g2_ref[0] * y
    if final:
        s_new = _rms(s_new) * fg_ref[...]
    o_ref[0] = s_new


def _combine(pos, meta, s, g2, final_g, ys, *, final):
    b, ntok, d = s.shape
    nseg = ntok // SEG
    off = 1 if final else 0
    nout = nseg - off
    kern = functools.partial(_combine_kernel, final=final)
    return pl.pallas_call(
        kern,
        grid=(b, nout),
        in_specs=[
            pl.BlockSpec((1, 1, 2 * SEG), lambda bb, i: (bb * nseg + i + off, 0, 0)),
            pl.BlockSpec((1, SEG, LANES), lambda bb, i: (bb, i + off, 0)),
            pl.BlockSpec((1, SEG, d), lambda bb, i: (bb, i + off, 0)),
            pl.BlockSpec((1, 1, d), lambda bb, i: (_mod_row(bb, i + off), 0, 0)),
            pl.BlockSpec((1, d), lambda bb, i: (0, 0)),
            pl.BlockSpec(memory_space=pl.ANY),
        ],
        out_specs=pl.BlockSpec((1, SEG, d), lambda bb, i: (bb, i, 0)),
        out_shape=jax.ShapeDtypeStruct((b, nout * SEG, d), F32),
        scratch_shapes=[
            pltpu.SMEM((2 * SEG,), jnp.int32),
            pltpu.VMEM((2, SEG, d // 2), jnp.uint32),
            pltpu.SemaphoreType.DMA,
            pltpu.SemaphoreType.DMA,
        ],
        compiler_params=_cparams(("arbitrary", "arbitrary")),
        name="moe_combine",
    )(pos, meta, s, g2, final_g, ys)


def _win_kernel(s_ref, g_ref, sh_ref, sc_ref, w_ref, gate_ref, u_ref):
    d = s_ref.shape[-1]
    h = _modulate(s_ref[0], g_ref[...], sh_ref[0], sc_ref[0]).astype(BF16)
    gate_ref[0] = jax.nn.gelu(jnp.dot(h, w_ref[:, :d], preferred_element_type=F32))
    u_ref[0] = jnp.dot(h, w_ref[:, d:], preferred_element_type=F32)


def _win(s, g, sh, sc, w):
    b, ntok, d = s.shape
    mod_spec = pl.BlockSpec((1, 1, d), lambda bb, i: (_mod_row(bb, i), 0, 0))
    seg_spec = pl.BlockSpec((1, SEG, d), lambda bb, i: (bb, i, 0))
    return pl.pallas_call(
        _win_kernel,
        grid=(b, ntok // SEG),
        in_specs=[
            seg_spec,
            pl.BlockSpec((1, d), lambda bb, i: (0, 0)),
            mod_spec, mod_spec,
            pl.BlockSpec((d, 2 * d), lambda bb, i: (0, 0)),
        ],
        out_specs=[seg_spec, seg_spec],
        out_shape=[jax.ShapeDtypeStruct(s.shape, F32), jax.ShapeDtypeStruct(s.shape, F32)],
        compiler_params=_cparams(("arbitrary", "arbitrary")),
        name="lru_in_proj",
    )(s, g, sh, sc, w)


def _lru_block(direction, step, nseg):
    return jnp.where((direction == 0) | (step == 0), step, nseg - step)


def _lru_kernel(u_ref, up_ref, un_ref, cw_ref, cb_ref, gw_ref, gb_ref, ap_ref, h_ref,
                a_ref, b_ref, st_ref, *, nseg):
    direction = pl.program_id(0)
    step = pl.program_id(2)
    blk = _lru_block(direction, step, nseg)
    d = u_ref.shape[-1]

    u = u_ref[0]
    has_prev = blk > 1
    has_next = (blk > 0) & (blk < nseg - 1)
    zero_row = jnp.zeros((1, d), F32)
    prev2 = jnp.where(has_prev, up_ref[0, 6:7, :], zero_row)
    prev1 = jnp.where(has_prev, up_ref[0, 7:8, :], zero_row)
    next1 = jnp.where(has_next, un_ref[0, 0:1, :], zero_row)
    rows = lax.broadcasted_iota(jnp.int32, u.shape, 0)
    um1 = jnp.where(rows == 0, prev1, pltpu.roll(u, 1, 0))
    um2 = jnp.where(rows == 0, prev2, jnp.where(rows == 1, prev1, pltpu.roll(u, 2, 0)))
    up1 = jnp.where(rows == SEG - 1, next1, pltpu.roll(u, SEG - 1, 0))
    cw = cw_ref[...]
    uc = cw[0:1] * um2 + cw[1:2] * um1 + cw[2:3] * u + cw[3:4] * up1 + cb_ref[...]

    ub = uc.astype(BF16)
    blkw = d // N_LRU_BLOCKS
    gr, gi = [], []
    for kb in range(N_LRU_BLOCKS):
        g = jnp.dot(ub[:, kb * blkw:(kb + 1) * blkw], gw_ref[0, kb], preferred_element_type=F32)
        gr.append(g[:, :blkw])
        gi.append(g[:, blkw:])
    gb = gb_ref[0]
    r_t = jax.nn.sigmoid(jnp.concatenate(gr, axis=1) + gb[0:1])
    i_t = jax.nn.sigmoid(jnp.concatenate(gi, axis=1) + gb[1:2])
    ap = ap_ref[0]
    log_sig = jnp.minimum(ap, 0.0) - jnp.log1p(jnp.exp(-jnp.abs(ap)))
    log_a = RG_C * r_t * log_sig
    a_t = jnp.exp(log_a)
    a_ref[...] = a_t
    b_ref[...] = jnp.sqrt(1.0 - a_t * a_t) * (i_t * uc)

    @pl.when(step == 0)
    def _():
        st_ref[...] = jnp.zeros(st_ref.shape, F32)

    def run(reverse):
        def body(t, hprev):
            tt = SEG - 1 - t if reverse else t
            hnew = a_ref[pl.ds(tt, 1), :] * hprev + b_ref[pl.ds(tt, 1), :]
            h_ref[0, 0, pl.ds(tt, 1), :] = hnew
            return hnew
        st_ref[...] = lax.fori_loop(0, SEG, body, st_ref[...], unroll=8)

    @pl.when(direction == 0)
    def _():
        run(False)

    @pl.when(direction == 1)
    def _():
        run(True)


def _lru(u, conv_w, conv_b, gate_w, gate_b, a_param):
    b, ntok, d = u.shape
    nseg = ntok // SEG
    sub = SEG // 8
    nsub = ntok // 8
    blkw = d // N_LRU_BLOCKS

    def blk(dd, i):
        return _lru_block(dd, i, nseg)

    kern = functools.partial(_lru_kernel, nseg=nseg)
    return pl.pallas_call(
        kern,
        grid=(2, b, nseg),
        in_specs=[
            pl.BlockSpec((1, SEG, d), lambda dd, bb, i: (bb, blk(dd, i), 0)),
            pl.BlockSpec((1, 8, d), lambda dd, bb, i: (bb, jnp.maximum(blk(dd, i) * sub - 1, 0), 0)),
            pl.BlockSpec((1, 8, d), lambda dd, bb, i: (bb, jnp.minimum((blk(dd, i) + 1) * sub, nsub - 1), 0)),
            pl.BlockSpec((4, d), lambda dd, bb, i: (0, 0)),
            pl.BlockSpec((1, d), lambda dd, bb, i: (0, 0)),
            pl.BlockSpec((1, N_LRU_BLOCKS, blkw, 2 * blkw), lambda dd, bb, i: (dd, 0, 0, 0)),
            pl.BlockSpec((1, 2, d), lambda dd, bb, i: (dd, 0, 0)),
            pl.BlockSpec((1, 1, d), lambda dd, bb, i: (dd, 0, 0)),
        ],
        out_specs=pl.BlockSpec((1, 1, SEG, d), lambda dd, bb, i: (dd, bb, blk(dd, i), 0)),
        out_shape=jax.ShapeDtypeStruct((2, b, ntok, d), F32),
        scratch_shapes=[
            pltpu.VMEM((SEG, d), F32),
            pltpu.VMEM((SEG, d), F32),
            pltpu.VMEM((1, d), F32),
        ],
        compiler_params=_cparams(("arbitrary", "arbitrary", "arbitrary")),
        name="rglru_scan",
    )(u, u, u, conv_w, conv_b, gate_w, gate_b, a_param)


def _rope_tables(n_ctx, n_lat):
    rows = n_lat // GRID_W
    row = jnp.repeat(jnp.arange(rows, dtype=F32), GRID_W)
    col = jnp.tile(jnp.arange(GRID_W, dtype=F32), rows)
    inv_freq = 1.0 / (ROPE_BASE ** (jnp.arange(0, ROPE_AXIS_DIM, 2, dtype=F32) / ROPE_AXIS_DIM))
    ang = jnp.stack([row[:, None] * inv_freq, col[:, None] * inv_freq], axis=1)
    ang = jnp.broadcast_to(ang[:, :, None, :], (n_lat, 2, 2, ROPE_AXIS_DIM // 2)).reshape(n_lat, HEAD_DIM)
    ang = jnp.concatenate([jnp.zeros((n_ctx, HEAD_DIM), F32), ang], axis=0)
    cos = jnp.cos(ang)
    sin = jnp.sin(ang)
    lane = jnp.arange(HEAD_DIM)
    sin = jnp.where((lane % ROPE_AXIS_DIM) < (ROPE_AXIS_DIM // 2), -sin, sin)
    return jnp.tile(cos, (1, 2)), jnp.tile(sin, (1, 2))


def kernel(x, c, ctx, c_ctx, mod_w, mod_b, norm_mix_g, norm_ffn_g, attn_w_qkv, attn_w_o, attn_lambda, attn_subln_g, lru_w_in, lru_conv_w, lru_conv_b, lru_gate_w, lru_gate_b, lru_a_param, lru_w_out, ffn_w_gate_up, ffn_w_down, moe_router_w, moe_w_gate_up, moe_w_down, final_norm_g):
    bsz, n_lat, d = x.shape
    n_ctx = ctx.shape[1]
    depth = mod_w.shape[0]
    assert n_ctx == SEG and n_lat % ATT_TQ == 0 and bsz <= 4
    ntok = n_ctx + n_lat
    assert ntok % ATT_TK == 0 and ntok % FFN_TM == 0
    nseg = ntok // SEG
    t_all = bsz * ntok

    s = jnp.concatenate([ctx, x], axis=1)
    c_pad = jnp.zeros((8, d), F32).at[:bsz].set(c).at[4].set(c_ctx)
    mod = _modulation(c_pad, mod_w, mod_b)
    cos, sin = _rope_tables(n_ctx, n_lat)
    tri = (jnp.arange(SEG)[:, None] > jnp.arange(SEG)[None, :]).astype(BF16)
    blkw = d // N_LRU_BLOCKS
    out = None

    for i in range(depth):
        j = i // 2
        last = i == depth - 1
        sh1, sc1, g1, sh2, sc2, g2 = [mod[i, :, k * d:(k + 1) * d].reshape(8, 1, d) for k in range(6)]
        ng_mix = norm_mix_g[i].reshape(1, d)
        ng_ffn = norm_ffn_g[i].reshape(1, d)
        if i % 2 == 0:
            lambda_init = 0.8 - 0.6 * math.exp(-0.3 * i)
            qt, k, vt = _qkv(s, ng_mix, sh1, sc1, attn_w_qkv[j].astype(BF16), cos, sin)
            subln = attn_subln_g[j].reshape(1, HEAD_W)
            o_ctx = _attention(attn_lambda[j], subln, qt, k, vt, q_off=0, nq=n_ctx,
                               n_chunks=1, tk=SEG, lambda_init=lambda_init)
            o_lat = _attention(attn_lambda[j], subln, qt, k, vt, q_off=n_ctx, nq=n_lat,
                               n_chunks=ntok // ATT_TK, tk=ATT_TK, lambda_init=lambda_init)
            s, f = _wo(o_ctx, o_lat, s, attn_w_o[j].astype(BF16), g1, ng_ffn, sh2, sc2)
            s = _ffn_dense(f, ffn_w_gate_up[j].astype(BF16)[None], ffn_w_down[j].astype(BF16)[None], s, g2)
        else:
            gate, u = _win(s, ng_mix, sh1, sc1, lru_w_in[j].astype(BF16))
            gw = lru_gate_w[j].transpose(0, 2, 3, 1, 4).reshape(2, N_LRU_BLOCKS, blkw, 2 * blkw).astype(BF16)
            hs = _lru(u, lru_conv_w[j], lru_conv_b[j].reshape(1, d), gw, lru_gate_b[j],
                      lru_a_param[j].reshape(2, 1, d))
            rw = jnp.zeros((d, LANES), BF16).at[:, :N_EXPERTS].set(moe_router_w[j].astype(BF16))
            s, f, meta, cnt = _wout(hs, gate, s, lru_w_out[j].astype(BF16), g1, ng_ffn, sh2, sc2, rw, tri)

            meta2 = meta.reshape(t_all, LANES)
            idx = meta2[:, 0:2].astype(jnp.int32)
            rank = meta2[:, 4:6].astype(jnp.int32)
            counts = cnt[0, :N_EXPERTS].astype(jnp.int32)
            padded = ((counts + MOE_TM - 1) // MOE_TM) * MOE_TM
            ends = jnp.cumsum(padded)
            starts = ends - padded
            pos = (starts[idx] + rank).reshape(t_all // SEG, 1, 2 * SEG)
            p_rows = 2 * t_all + N_EXPERTS * MOE_TM
            tile_start = jnp.arange(p_rows // MOE_TM, dtype=jnp.int32) * MOE_TM
            tile_expert = jnp.minimum(jnp.sum((tile_start[:, None] >= ends[None, :]).astype(jnp.int32), axis=1),
                                      N_EXPERTS - 1)
            n_tiles = (ends[-1:] // MOE_TM).astype(jnp.int32)

            xs = _dispatch(pos, f.reshape(t_all, d // 2), jnp.zeros((p_rows, d // 2), jnp.uint32))
            ys = _ffn_moe(tile_expert, n_tiles, xs, moe_w_gate_up[j].astype(BF16), moe_w_down[j].astype(BF16))
            res = _combine(pos, meta, s, g2, final_norm_g.reshape(1, d), ys, final=last)
            if last:
                out = res
            else:
                s = res
    return out
```
